```python
import jax
import jax.numpy as jnp
from jax import lax
import numpy as np

D_MODEL = 4096
BATCH = 4
SEQ = 2048
DEPTH = 1
DEC_BATCH = 128
DEC_SEQ = 1
PAST_LEN = 2048
PAGE_SIZE = 128

HEAD_DIM = 128
N_HEADS_SB = D_MODEL // (2 * HEAD_DIM)
N_HEADS_NSA = D_MODEL // (2 * HEAD_DIM)
N_KV_NSA = 2
NSA_GROUP = N_HEADS_NSA // N_KV_NSA
CMP_BLOCK = 32
CMP_STRIDE = 16
CMP_HIDDEN = 256
SEL_BLOCK = 64
SEL_TOP_N = 16
WINDOW = 512
D_FF = -(-(8 * D_MODEL) // (3 * 256)) * 256
Q_BLOCK = 128
ROPE_THETA = 10000.0
RMS_EPS = 1e-6
NEG_INF = -1e30
BIG = 1e30

kernel_name = "hybrid_stickbreak_nsa_decoder_step"


def _split_sizes():
    d_sb = N_HEADS_SB * HEAD_DIM
    d_nsa = N_HEADS_NSA * HEAD_DIM
    d_kv = N_KV_NSA * HEAD_DIM
    return (d_sb, d_sb, d_sb, d_nsa, d_kv, d_kv, d_kv, d_kv, d_kv, d_kv, 3 * N_HEADS_NSA, 2 * D_MODEL)


def _rmsnorm(x, g):
    xf = x.astype(jnp.float32)
    inv = lax.rsqrt(jnp.mean(xf * xf, axis=-1, keepdims=True) + RMS_EPS)
    return (xf * inv * g.astype(jnp.float32)).astype(x.dtype)


def _rope(x, pos):
    half = HEAD_DIM // 2
    inv_freq = ROPE_THETA ** (-jnp.arange(half, dtype=jnp.float32) / half)
    ang = pos.astype(jnp.float32)[:, None] * inv_freq[None, :]
    cos = jnp.cos(ang)[None, :, None, :]
    sin = jnp.sin(ang)[None, :, None, :]
    xf = x.astype(jnp.float32)
    x1, x2 = xf[..., :half], xf[..., half:]
    return jnp.concatenate([x1 * cos - x2 * sin, x2 * cos + x1 * sin], axis=-1).astype(x.dtype)


def _masked_softmax(s, mask):
    s = jnp.where(mask, s, NEG_INF)
    m = jnp.max(s, axis=-1, keepdims=True)
    e = jnp.where(mask, jnp.exp(s - m), 0.0)
    return e / jnp.maximum(jnp.sum(e, axis=-1, keepdims=True), 1e-30)


def _query_block(tq):
    return min(Q_BLOCK, tq)


def _sweep_query_blocks(fn, q_pos, *qs):
    tq = q_pos.shape[0]
    qb = _query_block(tq)
    nb = -(-tq // qb)
    pad = nb * qb - tq
    pos = jnp.pad(q_pos, (0, pad), mode="edge").reshape(nb, qb)
    blocks = tuple(
        jnp.moveaxis(
            jnp.pad(q, [(0, 0), (0, pad)] + [(0, 0)] * (q.ndim - 2)).reshape(q.shape[0], nb, qb, *q.shape[2:]),
            1, 0)
        for q in qs)
    out = lax.map(lambda a: fn(a[0], *a[1]), (pos, blocks))
    out = jnp.moveaxis(out, 0, 1)
    out = out.reshape(out.shape[0], nb * qb, *out.shape[3:])
    return out[:, :tq]


def _project(h, pos, w_in):
    b, t, _ = h.shape
    z = jnp.einsum("btd,de->bte", h, w_in)
    cuts = [int(c) for c in np.cumsum(_split_sizes())[:-1]]
    (q_sb, k_sb, v_sb, q_nsa, k_cmp, v_cmp, k_sel, v_sel, k_win, v_win,
     nsa_gate, merge_gate) = jnp.split(z, cuts, axis=-1)

    def heads(a):
        return a.reshape(b, t, -1, HEAD_DIM)

    return (heads(q_sb), heads(k_sb), heads(v_sb), _rope(heads(q_nsa), pos),
            _rope(heads(k_cmp), pos), heads(v_cmp), _rope(heads(k_sel), pos), heads(v_sel),
            _rope(heads(k_win), pos), heads(v_win),
            nsa_gate.reshape(b, t, N_HEADS_NSA, 3), merge_gate)


def _stick_breaking_attention(q, k, v, q_pos):
    k_pos = jnp.arange(k.shape[1], dtype=jnp.int32)
    scale = HEAD_DIM ** -0.5

    def block(pos, qb):
        z = jnp.einsum("bqhd,bkhd->bhqk", qb, k).astype(jnp.float32) * scale
        causal = k_pos[None, :] < pos[:, None]
        log_1m = jnp.where(causal, jax.nn.log_sigmoid(-z), 0.0)
        tail = lax.cumsum(log_1m, axis=3, reverse=True) - log_1m
        a = jnp.where(causal, jnp.exp(jax.nn.log_sigmoid(z) + tail), 0.0)
        return jnp.einsum("bhqk,bkhd->bqhd", a.astype(v.dtype), v)

    return _sweep_query_blocks(block, q_pos, q)


def _compress(rows, pe, w1, w2):
    b, tk = rows.shape[:2]
    n_cmp = (tk - CMP_BLOCK) // CMP_STRIDE + 1
    idx = jnp.arange(n_cmp)[:, None] * CMP_STRIDE + jnp.arange(CMP_BLOCK)[None, :]
    blk = rows[:, idx] + pe[None, None, :, None, :]
    flat = blk.transpose(0, 1, 3, 2, 4).reshape(b, n_cmp, N_KV_NSA, CMP_BLOCK * HEAD_DIM)
    return jax.nn.gelu(flat @ w1) @ w2


def _native_sparse_attention(q, gate_logits, k_cmp_rows, v_cmp_rows, k_sel, v_sel,
                             k_win, v_win, win_pos0, q_pos, lw):
    b, tk = k_sel.shape[:2]
    scale = HEAD_DIM ** -0.5
    kc = _compress(k_cmp_rows, lw["cmp_k_pe"], lw["cmp_k_w1"], lw["cmp_k_w2"])
    vc = _compress(v_cmp_rows, lw["cmp_v_pe"], lw["cmp_v_w1"], lw["cmp_v_w2"])
    n_cmp = kc.shape[1]
    cmp_start = jnp.arange(n_cmp, dtype=jnp.int32) * CMP_STRIDE
    cmp_end = cmp_start + CMP_BLOCK - 1
    n_sel = -(-tk // SEL_BLOCK)
    pad = n_sel * SEL_BLOCK - tk

    def to_blocks(a):
        a = jnp.pad(a, ((0, 0), (0, pad), (0, 0), (0, 0)))
        return a.reshape(b, n_sel, SEL_BLOCK, N_KV_NSA, HEAD_DIM).transpose(0, 3, 1, 2, 4)

    ks_blk, vs_blk = to_blocks(k_sel), to_blocks(v_sel)
    sel_ids = jnp.arange(n_sel, dtype=jnp.int32)
    sel_start = sel_ids * SEL_BLOCK
    overlap = ((cmp_start[:, None] < sel_start[None, :] + SEL_BLOCK)
               & (cmp_start[:, None] + CMP_BLOCK > sel_start[None, :])).astype(jnp.float32)
    n_top = min(SEL_TOP_N, n_sel)
    qb_len = _query_block(q_pos.shape[0])
    band = WINDOW + qb_len
    kw_pad = jnp.pad(k_win, ((0, 0), (WINDOW, qb_len), (0, 0), (0, 0)))
    vw_pad = jnp.pad(v_win, ((0, 0), (WINDOW, qb_len), (0, 0), (0, 0)))
    b_ix = jnp.arange(b)[:, None, None, None]
    g_ix = jnp.arange(N_KV_NSA)[None, :, None, None]

    def block(pos, qb, gb):
        nq = pos.shape[0]
        qg = qb.reshape(b, nq, N_KV_NSA, NSA_GROUP, HEAD_DIM)
        s_c = jnp.einsum("bqgrd,bngd->bgrqn", qg, kc).astype(jnp.float32) * scale
        p_c = _masked_softmax(s_c, cmp_end[None, :] <= pos[:, None])
        o_c = jnp.einsum("bgrqn,bngd->bqgrd", p_c.astype(vc.dtype), vc)
        imp = jnp.einsum("bgrqn,nm->bgqm", p_c, overlap)
        cur = pos // SEL_BLOCK
        forced = ((sel_ids[None, :] == 0) | (sel_ids[None, :] == cur[:, None])
                  | (sel_ids[None, :] == cur[:, None] - 1))
        imp = jnp.where(forced, BIG, imp)
        imp = jnp.where(sel_ids[None, :] <= cur[:, None], imp, -BIG)
        top_val, top_idx = lax.top_k(imp, n_top)
        ks_g = ks_blk[b_ix, g_ix, top_idx].reshape(b, N_KV_NSA, nq, n_top * SEL_BLOCK, HEAD_DIM)
        vs_g = vs_blk[b_ix, g_ix, top_idx].reshape(b, N_KV_NSA, nq, n_top * SEL_BLOCK, HEAD_DIM)
        k_pos_sel = top_idx[..., None] * SEL_BLOCK + jnp.arange(SEL_BLOCK, dtype=jnp.int32)
        vis_s = ((top_val > -0.5 * BIG)[..., None]
                 & (k_pos_sel <= pos[None, None, :, None, None])).reshape(b, N_KV_NSA, nq, n_top * SEL_BLOCK)
        s_s = jnp.einsum("bqgrd,bgqkd->bgrqk", qg, ks_g).astype(jnp.float32) * scale
        p_s = _masked_softmax(s_s, vis_s[:, :, None])
        o_s = jnp.einsum("bgrqk,bgqkd->bqgrd", p_s.astype(vs_g.dtype), vs_g)
        start = pos[0] - win_pos0
        kw_b = lax.dynamic_slice_in_dim(kw_pad, start, band, axis=1)
        vw_b = lax.dynamic_slice_in_dim(vw_pad, start, band, axis=1)
        kw_pos = pos[0] - WINDOW + jnp.arange(band, dtype=jnp.int32)
        vis_w = ((kw_pos[None, :] <= pos[:, None]) & (kw_pos[None, :] >= pos[:, None] - WINDOW)
                 & (kw_pos[None, :] >= win_pos0))
        s_w = jnp.einsum("bqgrd,bkgd->bgrqk", qg, kw_b).astype(jnp.float32) * scale
        p_w = _masked_softmax(s_w, vis_w)
        o_w = jnp.einsum("bgrqk,bkgd->bqgrd", p_w.astype(vw_b.dtype), vw_b)
        gt = jax.nn.sigmoid(gb.astype(jnp.float32)).reshape(b, nq, N_KV_NSA, NSA_GROUP, 3)
        o = gt[..., 0:1] * o_c + gt[..., 1:2] * o_s + gt[..., 2:3] * o_w
        return o.reshape(b, nq, N_HEADS_NSA, HEAD_DIM).astype(qb.dtype)

    return _sweep_query_blocks(block, q_pos, q, gate_logits)


def _layer_update(x, q_pos, q_sb, k_sb, v_sb, q_nsa, nsa_gate, k_cmp, v_cmp, k_sel, v_sel,
                  k_win, v_win, win_pos0, merge_gate, lw):
    b, t, _ = x.shape
    o_sb = _stick_breaking_attention(q_sb, k_sb, v_sb, q_pos).reshape(b, t, -1)
    o_nsa = _native_sparse_attention(q_nsa, nsa_gate, k_cmp, v_cmp, k_sel, v_sel,
                                     k_win, v_win, win_pos0, q_pos, lw).reshape(b, t, -1)
    y_sb = jnp.einsum("bte,ed->btd", o_sb, lw["w_branch_sb"])
    y_nsa = jnp.einsum("bte,ed->btd", o_nsa, lw["w_branch_nsa"])
    g_sb, g_nsa = jnp.split(jax.nn.sigmoid(merge_gate.astype(jnp.float32)), 2, axis=-1)
    merged = (g_sb * y_sb + g_nsa * y_nsa).astype(x.dtype)
    x = x + jnp.einsum("btd,de->bte", merged, lw["w_out"])
    h = _rmsnorm(x, lw["g_ffn"])
    ff = jax.nn.silu(h @ lw["w_gate"]) * (h @ lw["w_up"])
    return x + ff @ lw["w_down"]


def _with_past(pool, layer, page_table, slot, new_rows):
    rows = pool[layer, page_table, :, slot]
    rows = rows.reshape(rows.shape[0], rows.shape[1] * rows.shape[2], *rows.shape[3:])
    return jnp.concatenate([rows, new_rows.astype(rows.dtype)], axis=1)


def setup_inputs(seed: int = 0) -> dict:
    key = jax.random.key(seed)
    ks = jax.random.split(key, 24)
    f32 = jnp.float32
    n_pages = PAST_LEN // PAGE_SIZE
    n_used = DEC_BATCH * n_pages
    n_pool = n_used + (n_used + 3) // 4
    w_buf = min(WINDOW, PAST_LEN)
    d_in = int(sum(_split_sizes()))
    d_sb = N_HEADS_SB * HEAD_DIM
    d_nsa = N_HEADS_NSA * HEAD_DIM

    def nrm(k, shape, scale):
        return jax.random.normal(k, shape, f32) * scale

    page_table = jax.random.permutation(ks[5], n_pool)[:n_used].reshape(DEC_BATCH, n_pages).astype(jnp.int32)
    return {
        "x_prompt": nrm(ks[0], (BATCH, SEQ, D_MODEL), 1.0),
        "x_sample": nrm(ks[1], (DEC_BATCH, DEC_SEQ, D_MODEL), 1.0),
        "cache_sb_kv": nrm(ks[2], (DEPTH, n_pool, PAGE_SIZE, 2, N_HEADS_SB, HEAD_DIM), 1.0),
        "cache_nsa_kv": nrm(ks[3], (DEPTH, n_pool, PAGE_SIZE, 4, N_KV_NSA, HEAD_DIM), 1.0),
        "state_nsa_win": nrm(ks[4], (DEPTH, DEC_BATCH, w_buf, 2, N_KV_NSA, HEAD_DIM), 1.0),
        "page_table": page_table,
        "g_attn": 1.0 + nrm(ks[6], (DEPTH, D_MODEL), 0.01),
        "w_in": nrm(ks[7], (DEPTH, D_MODEL, d_in), D_MODEL ** -0.5),
        "cmp_k_pe": nrm(ks[8], (DEPTH, CMP_BLOCK, HEAD_DIM), 0.1),
        "cmp_k_w1": nrm(ks[9], (DEPTH, CMP_BLOCK * HEAD_DIM, CMP_HIDDEN), (CMP_BLOCK * HEAD_DIM) ** -0.5),
        "cmp_k_w2": nrm(ks[10], (DEPTH, CMP_HIDDEN, HEAD_DIM), CMP_HIDDEN ** -0.5),
        "cmp_v_pe": nrm(ks[11], (DEPTH, CMP_BLOCK, HEAD_DIM), 0.1),
        "cmp_v_w1": nrm(ks[12], (DEPTH, CMP_BLOCK * HEAD_DIM, CMP_HIDDEN), (CMP_BLOCK * HEAD_DIM) ** -0.5),
        "cmp_v_w2": nrm(ks[13], (DEPTH, CMP_HIDDEN, HEAD_DIM), CMP_HIDDEN ** -0.5),
        "w_branch_sb": nrm(ks[14], (DEPTH, d_sb, D_MODEL), d_sb ** -0.5),
        "w_branch_nsa": nrm(ks[15], (DEPTH, d_nsa, D_MODEL), d_nsa ** -0.5),
        "w_out": nrm(ks[16], (DEPTH, D_MODEL, D_MODEL), D_MODEL ** -0.5),
        "g_ffn": 1.0 + nrm(ks[17], (DEPTH, D_MODEL), 0.01),
        "w_gate": nrm(ks[18], (DEPTH, D_MODEL, D_FF), D_MODEL ** -0.5),
        "w_up": nrm(ks[19], (DEPTH, D_MODEL, D_FF), D_MODEL ** -0.5),
        "w_down": nrm(ks[20], (DEPTH, D_FF, D_MODEL), D_FF ** -0.5),
        "g_final": 1.0 + nrm(ks[21], (D_MODEL,), 0.01),
    }


def reference(x_prompt, x_sample, cache_sb_kv, cache_nsa_kv, state_nsa_win, page_table,
              g_attn, w_in, cmp_k_pe, cmp_k_w1, cmp_k_w2, cmp_v_pe, cmp_v_w1, cmp_v_w2,
              w_branch_sb, w_branch_nsa, w_out, g_ffn, w_gate, w_up, w_down, g_final):
    pos_p = jnp.arange(SEQ, dtype=jnp.int32)
    pos_s = PAST_LEN + jnp.arange(DEC_SEQ, dtype=jnp.int32)
    w_buf = state_nsa_win.shape[2]
    win_p_len = min(WINDOW, SEQ)
    xp, xs = x_prompt, x_sample
    sb_p, sb_s, nsa_p, nsa_s, win_p, win_s = [], [], [], [], [], []
    for layer in range(DEPTH):
        lw = {
            "cmp_k_pe": cmp_k_pe[layer], "cmp_k_w1": cmp_k_w1[layer], "cmp_k_w2": cmp_k_w2[layer],
            "cmp_v_pe": cmp_v_pe[layer], "cmp_v_w1": cmp_v_w1[layer], "cmp_v_w2": cmp_v_w2[layer],
            "w_branch_sb": w_branch_sb[layer], "w_branch_nsa": w_branch_nsa[layer],
            "w_out": w_out[layer], "g_ffn": g_ffn[layer],
            "w_gate": w_gate[layer], "w_up": w_up[layer], "w_down": w_down[layer],
        }
        (q_sb, k_sb, v_sb, q_nsa, k_cmp, v_cmp, k_sel, v_sel, k_win, v_win,
         nsa_gate, merge_gate) = _project(_rmsnorm(xp, g_attn[layer]), pos_p, w_in[layer])
        xp = _layer_update(xp, pos_p, q_sb, k_sb, v_sb, q_nsa, nsa_gate, k_cmp, v_cmp, k_sel, v_sel,
                           k_win, v_win, 0, merge_gate, lw)
        sb_p.append(jnp.stack([k_sb, v_sb], axis=2))
        nsa_p.append(jnp.stack([k_cmp, v_cmp, k_sel, v_sel], axis=2))
        win_p.append(jnp.stack([k_win, v_win], axis=2)[:, SEQ - win_p_len:])
        (q_sb, k_sb, v_sb, q_nsa, k_cmp, v_cmp, k_sel, v_sel, k_win, v_win,
         nsa_gate, merge_gate) = _project(_rmsnorm(xs, g_attn[layer]), pos_s, w_in[layer])
        new_win = jnp.stack([k_win, v_win], axis=2)
        win_rows = jnp.concatenate([state_nsa_win[layer], new_win.astype(state_nsa_win.dtype)], axis=1)
        xs = _layer_update(
            xs, pos_s, q_sb,
            _with_past(cache_sb_kv, layer, page_table, 0, k_sb),
            _with_past(cache_sb_kv, layer, page_table, 1, v_sb),
            q_nsa, nsa_gate,
            _with_past(cache_nsa_kv, layer, page_table, 0, k_cmp),
            _with_past(cache_nsa_kv, layer, page_table, 1, v_cmp),
            _with_past(cache_nsa_kv, layer, page_table, 2, k_sel),
            _with_past(cache_nsa_kv, layer, page_table, 3, v_sel),
            win_rows[:, :, 0], win_rows[:, :, 1], PAST_LEN - w_buf, merge_gate, lw)
        sb_s.append(jnp.stack([k_sb, v_sb], axis=2))
        nsa_s.append(jnp.stack([k_cmp, v_cmp, k_sel, v_sel], axis=2))
        win_s.append(win_rows[:, win_rows.shape[1] - w_buf:])
    return (_rmsnorm(xp, g_final), _rmsnorm(xs, g_final),
            jnp.stack(sb_p), jnp.stack(sb_s), jnp.stack(nsa_p), jnp.stack(nsa_s),
            jnp.stack(win_p), jnp.stack(win_s))
```

```python
import functools

import jax
import jax.numpy as jnp
from jax import lax
from jax.experimental import pallas as pl
from jax.experimental.pallas import tpu as pltpu

F32 = jnp.float32
BF16 = jnp.bfloat16
I32 = jnp.int32

HEAD_DIM = 128
N_HEADS_SB = 16
N_HEADS_NSA = 16
N_KV_NSA = 2
NSA_GROUP = N_HEADS_NSA // N_KV_NSA
CMP_BLOCK = 32
CMP_STRIDE = 16
SEL_BLOCK = 64
SEL_TOP_N = 16
WINDOW = 512
ROPE_THETA = 10000.0
RMS_EPS = 1e-6
NEG_INF = -1e30
BIG = 1e30
LANES = 128
VMEM_LIMIT_BYTES = 48 * 2**20
FF_PAD_MULTIPLE = 1024


def _params(*sem):
    return pltpu.CompilerParams(dimension_semantics=sem, vmem_limit_bytes=VMEM_LIMIT_BYTES)


def _dot(a, b):
    return jnp.dot(a, b, preferred_element_type=F32)


def _dot_nt(a, b):
    return lax.dot_general(a, b, (((1,), (1,)), ((), ())), preferred_element_type=F32)


def _lanes(x, n):
    return x if n == LANES else jnp.concatenate([x] * (n // LANES), axis=1)


def _split2(x):
    hi = x.astype(BF16)
    lo = (x - hi.astype(F32)).astype(BF16)
    return hi, lo


def _split3(x):
    hi = x.astype(BF16)
    r1 = x - hi.astype(F32)
    mid = r1.astype(BF16)
    lo = (r1 - mid.astype(F32)).astype(BF16)
    return hi, mid, lo


def _rmsnorm_body(x_ref, g_ref, o_ref):
    x = x_ref[...]
    inv = lax.rsqrt(jnp.mean(x * x, axis=-1, keepdims=True) + RMS_EPS)
    o_ref[...] = (x * inv * g_ref[...]).astype(o_ref.dtype)


def _rmsnorm(x, g, out_dtype, tm):
    m, d = x.shape
    return pl.pallas_call(
        _rmsnorm_body,
        grid=(m // tm,),
        in_specs=[pl.BlockSpec((tm, d), lambda i: (i, 0)), pl.BlockSpec((1, d), lambda i: (0, 0))],
        out_specs=pl.BlockSpec((tm, d), lambda i: (i, 0)),
        out_shape=jax.ShapeDtypeStruct((m, d), out_dtype),
        compiler_params=_params("parallel"),
        name="rmsnorm",
    )(x, g.reshape(1, d))


def _rope_tile(y, cos, sin, rope_slabs):
    out = []
    for c, roped in enumerate(rope_slabs):
        x = y[:, c * LANES:(c + 1) * LANES]
        out.append(x * cos + pltpu.roll(x, HEAD_DIM // 2, 1) * sin if roped else x)
    return out[0] if len(out) == 1 else jnp.concatenate(out, axis=1)


def _mm_body(a_ref, w_ref, o_ref):
    o_ref[...] = _dot(a_ref[...], w_ref[...]).astype(o_ref.dtype)


def _mm_rope_body(a_ref, w_ref, cos_ref, sin_ref, o_ref, *, rope_slabs):
    y = _dot(a_ref[...], w_ref[...])
    o_ref[...] = _rope_tile(y, cos_ref[...], sin_ref[...], rope_slabs).astype(o_ref.dtype)


def _mm_res_body(a_ref, w_ref, r_ref, o_ref):
    @pl.when(pl.program_id(2) == 0)
    def _():
        o_ref[...] = r_ref[...]

    o_ref[...] += _dot(a_ref[...], w_ref[...])


def _mm_gate_up_body(a_ref, wg_ref, wu_ref, o_ref):
    a = a_ref[...]
    o_ref[...] = (jax.nn.silu(_dot(a, wg_ref[...])) * _dot(a, wu_ref[...])).astype(o_ref.dtype)


def _mm_merge_body(a1_ref, w1_ref, a2_ref, w2_ref, g1_ref, g2_ref, o_ref):
    y1 = _dot(a1_ref[...], w1_ref[...])
    y2 = _dot(a2_ref[...], w2_ref[...])
    o_ref[...] = (jax.nn.sigmoid(g1_ref[...]) * y1 + jax.nn.sigmoid(g2_ref[...]) * y2).astype(o_ref.dtype)


def _tile(n, pref):
    return pref if n % pref == 0 else n


def _matmul(a, w, out_dtype, *, tm, tn, name):
    m, k = a.shape
    n = w.shape[1]
    tm, tn = _tile(m, tm), _tile(n, tn)
    return pl.pallas_call(
        _mm_body,
        grid=(m // tm, n // tn),
        in_specs=[pl.BlockSpec((tm, k), lambda i, j: (i, 0)), pl.BlockSpec((k, tn), lambda i, j: (0, j))],
        out_specs=pl.BlockSpec((tm, tn), lambda i, j: (i, j)),
        out_shape=jax.ShapeDtypeStruct((m, n), out_dtype),
        compiler_params=_params("parallel", "arbitrary"),
        name=name,
    )(a, w)


def _matmul_rope(a, w, cos, sin, out_dtype, *, tm, tn, rope_slabs, name):
    m, k = a.shape
    n = w.shape[1]
    tm, tn = _tile(m, tm), _tile(n, tn)
    pos_tiles = cos.shape[0] // tm
    return pl.pallas_call(
        functools.partial(_mm_rope_body, rope_slabs=rope_slabs),
        grid=(m // tm, n // tn),
        in_specs=[pl.BlockSpec((tm, k), lambda i, j: (i, 0)), pl.BlockSpec((k, tn), lambda i, j: (0, j)),
                  pl.BlockSpec((tm, LANES), lambda i, j: (i % pos_tiles, 0)),
                  pl.BlockSpec((tm, LANES), lambda i, j: (i % pos_tiles, 0))],
        out_specs=pl.BlockSpec((tm, tn), lambda i, j: (i, j)),
        out_shape=jax.ShapeDtypeStruct((m, n), out_dtype),
        compiler_params=_params("parallel", "arbitrary"),
        name=name,
    )(a, w, cos, sin)


def _matmul_residual(a, w, res, *, tm, tn, tk, name):
    m, k = a.shape
    n = w.shape[1]
    tm, tn, tk = _tile(m, tm), _tile(n, tn), _tile(k, tk)
    return pl.pallas_call(
        _mm_res_body,
        grid=(m // tm, n // tn, k // tk),
        in_specs=[pl.BlockSpec((tm, tk), lambda i, j, kk: (i, kk)), pl.BlockSpec((tk, tn), lambda i, j, kk: (kk, j)),
                  pl.BlockSpec((tm, tn), lambda i, j, kk: (i, j))],
        out_specs=pl.BlockSpec((tm, tn), lambda i, j, kk: (i, j)),
        out_shape=jax.ShapeDtypeStruct((m, n), F32),
        compiler_params=_params("parallel", "arbitrary", "arbitrary"),
        name=name,
    )(a, w, res)


def _matmul_gate_up(a, wg, wu, *, tm, tn, name):
    m, k = a.shape
    n = wg.shape[1]
    tm, tn = _tile(m, tm), _tile(n, tn)
    return pl.pallas_call(
        _mm_gate_up_body,
        grid=(m // tm, n // tn),
        in_specs=[pl.BlockSpec((tm, k), lambda i, j: (i, 0)), pl.BlockSpec((k, tn), lambda i, j: (0, j)),
                  pl.BlockSpec((k, tn), lambda i, j: (0, j))],
        out_specs=pl.BlockSpec((tm, tn), lambda i, j: (i, j)),
        out_shape=jax.ShapeDtypeStruct((m, n), BF16),
        compiler_params=_params("parallel", "arbitrary"),
        name=name,
    )(a, wg, wu)


def _matmul_merge(a1, w1, a2, w2, gates, *, tm, tn, name):
    m, k = a1.shape
    n = w1.shape[1]
    tm, tn = _tile(m, tm), _tile(n, tn)
    nj = n // tn
    return pl.pallas_call(
        _mm_merge_body,
        grid=(m // tm, nj),
        in_specs=[pl.BlockSpec((tm, k), lambda i, j: (i, 0)), pl.BlockSpec((k, tn), lambda i, j: (0, j)),
                  pl.BlockSpec((tm, k), lambda i, j: (i, 0)), pl.BlockSpec((k, tn), lambda i, j: (0, j)),
                  pl.BlockSpec((tm, tn), lambda i, j: (i, j)), pl.BlockSpec((tm, tn), lambda i, j: (i, j + nj))],
        out_specs=pl.BlockSpec((tm, tn), lambda i, j: (i, j)),
        out_shape=jax.ShapeDtypeStruct((m, n), BF16),
        compiler_params=_params("parallel", "arbitrary"),
        name=name,
    )(a1, w1, a2, w2, gates, gates)


def _suffix_matrix(t):
    row = lax.broadcasted_iota(I32, (t, 2 * t), 0)
    col = lax.broadcasted_iota(I32, (t, 2 * t), 1)
    return jnp.where((row > col) | (col >= t), 1.0, 0.0).astype(BF16)


def _log_one_minus_beta(z):
    return -(jnp.maximum(z, 0.0) + jnp.log1p(jnp.exp(-jnp.abs(z))))


def _sb_block(z, v, u, carry, causal):
    t = z.shape[1]
    l1m = _log_one_minus_beta(z)
    if causal is not None:
        l1m = jnp.where(causal, l1m, 0.0)
    hi, lo = _split2(l1m)
    sums = _dot(hi, u) + _dot(lo, u)
    a = jnp.exp(l1m + z + sums[:, :t] + carry)
    if causal is not None:
        a = jnp.where(causal, a, 0.0)
    return _dot(a.astype(BF16), v), carry + sums[:, t:]


def _sb_prompt_body(q_ref, k_ref, v_ref, o_ref, *, tile, scale):
    seq = q_ref.shape[0]
    u = _suffix_matrix(tile)
    row = lax.broadcasted_iota(I32, (tile, tile), 0)
    col = lax.broadcasted_iota(I32, (tile, tile), 1)
    causal = col < row

    def load_kv(kj):
        k0 = pl.multiple_of(kj * tile, tile)
        return k_ref[pl.ds(k0, tile), :].astype(BF16), v_ref[pl.ds(k0, tile), :].astype(BF16)

    def q_tile(qi, _):
        q0 = pl.multiple_of(qi * tile, tile)
        q = q_ref[pl.ds(q0, tile), :]
        k, v = load_kv(qi)
        acc, carry = _sb_block(_dot_nt(q, k) * scale, v, u, jnp.zeros((tile, tile), F32), causal)

        def left(t, c):
            k, v = load_kv(qi - 1 - t)
            pv, carry = _sb_block(_dot_nt(q, k) * scale, v, u, c[1], None)
            return c[0] + pv, carry

        acc, _ = lax.fori_loop(0, qi, left, (acc, carry))
        o_ref[pl.ds(q0, tile), :] = acc.astype(o_ref.dtype)
        return 0

    lax.fori_loop(0, seq // tile, q_tile, 0)


def _sb_prompt(q, kv, *, batch, seq):
    h = N_HEADS_SB
    return pl.pallas_call(
        functools.partial(_sb_prompt_body, tile=LANES, scale=HEAD_DIM ** -0.5),
        grid=(batch, h),
        in_specs=[pl.BlockSpec((seq, HEAD_DIM), lambda b, i: (b, i)),
                  pl.BlockSpec((seq, HEAD_DIM), lambda b, i: (b, i)),
                  pl.BlockSpec((seq, HEAD_DIM), lambda b, i: (b, h + i))],
        out_specs=pl.BlockSpec((seq, HEAD_DIM), lambda b, i: (b, i)),
        out_shape=jax.ShapeDtypeStruct(q.shape, BF16),
        compiler_params=_params("parallel", "parallel"),
        name="sb_prompt",
    )(q, kv, kv)


def _sb_sample_body(pt_ref, q_ref, page_ref, o_ref, acc_ref, carry_ref, *, scale):
    del pt_ref
    pp = pl.program_id(1)
    d = N_HEADS_SB * HEAD_DIM
    t = page_ref.shape[1]

    @pl.when(pp == 0)
    def _():
        acc_ref[...] = jnp.zeros_like(acc_ref)
        carry_ref[...] = jnp.zeros_like(carry_ref)

    head = lax.broadcasted_iota(I32, (N_HEADS_SB, d), 0)
    lane_head = lax.broadcasted_iota(I32, (N_HEADS_SB, d), 1) // HEAD_DIM
    own = head == lane_head
    q_bd = jnp.where(own, jnp.broadcast_to(q_ref[0], (N_HEADS_SB, d)), 0.0).astype(BF16)
    k = page_ref[0, :, :d].astype(BF16)
    v = page_ref[0, :, d:].astype(BF16)
    z = _dot_nt(q_bd, k) * scale
    pv, carry = _sb_block(z, v, _suffix_matrix(t), carry_ref[...], None)
    acc_ref[...] += pv
    carry_ref[...] = carry

    @pl.when(pp == pl.num_programs(1) - 1)
    def _():
        o_ref[0] = jnp.sum(jnp.where(own, acc_ref[...], 0.0), axis=0, keepdims=True)


def _sb_sample(q, cache, page_table):
    s, n_pages = page_table.shape
    d = N_HEADS_SB * HEAD_DIM
    page = cache.shape[1]
    grid_spec = pltpu.PrefetchScalarGridSpec(
        num_scalar_prefetch=1,
        grid=(s, n_pages),
        in_specs=[pl.BlockSpec((1, 1, d), lambda b, p, pt: (b, 0, 0)),
                  pl.BlockSpec((1, page, 2 * d), lambda b, p, pt: (pt[b, n_pages - 1 - p], 0, 0))],
        out_specs=pl.BlockSpec((1, 1, d), lambda b, p, pt: (b, 0, 0)),
        scratch_shapes=[pltpu.VMEM((N_HEADS_SB, d), F32), pltpu.VMEM((N_HEADS_SB, page), F32)],
    )
    return pl.pallas_call(
        functools.partial(_sb_sample_body, scale=HEAD_DIM ** -0.5),
        grid_spec=grid_spec,
        out_shape=jax.ShapeDtypeStruct((s, 1, d), F32),
        compiler_params=_params("parallel", "arbitrary"),
        name="sb_sample",
    )(page_table, q, cache)


def _compress_rows(x, pe_ref, w1_ref, w2_ref, kind, n_cmp):
    c = LANES
    pa = _dot((x + pe_ref[kind, 0]).astype(BF16), w1_ref[kind, 0])
    pb = _dot((x + pe_ref[kind, 1]).astype(BF16), w1_ref[kind, 1])
    pb = jnp.concatenate([pltpu.roll(pb[i:i + c], c - 1, 0) for i in range(0, x.shape[0], c)], axis=0)
    out = _dot(jax.nn.gelu(pa + pb).astype(BF16), w2_ref[kind])
    row = lax.broadcasted_iota(I32, out.shape, 0) % c
    return jnp.where(row < n_cmp, out, 0.0)


def _compress_prompt_body(x_ref, pe_ref, w1_ref, w2_ref, o_ref, *, n_cmp):
    row_w = x_ref.shape[1] // CMP_STRIDE
    for kind in range(2):
        xs = []
        for g in range(N_KV_NSA):
            c0 = (kind * N_KV_NSA + g) * HEAD_DIM
            xs.append(jnp.concatenate(
                [x_ref[:, l * row_w + c0:l * row_w + c0 + HEAD_DIM] for l in range(CMP_STRIDE)], axis=1))
        out = _compress_rows(jnp.concatenate(xs, axis=0), pe_ref, w1_ref, w2_ref, kind, n_cmp)
        for g in range(N_KV_NSA):
            o_ref[0, kind, g] = out[g * LANES:(g + 1) * LANES]


def _compress_prompt(nkv, pe, w1, w2, *, batch, seq):
    chunks = seq // CMP_STRIDE
    assert chunks == LANES
    n_cmp = (seq - CMP_BLOCK) // CMP_STRIDE + 1
    x = nkv.reshape(batch * chunks, CMP_STRIDE * nkv.shape[1])
    return pl.pallas_call(
        functools.partial(_compress_prompt_body, n_cmp=n_cmp),
        grid=(batch,),
        in_specs=[pl.BlockSpec((chunks, x.shape[1]), lambda b: (b, 0)),
                  pl.BlockSpec(pe.shape, lambda b: (0, 0, 0, 0)),
                  pl.BlockSpec(w1.shape, lambda b: (0, 0, 0, 0)),
                  pl.BlockSpec(w2.shape, lambda b: (0, 0, 0))],
        out_specs=pl.BlockSpec((1, 2, N_KV_NSA, chunks, HEAD_DIM), lambda b: (b, 0, 0, 0, 0)),
        out_shape=jax.ShapeDtypeStruct((batch, 2, N_KV_NSA, chunks, HEAD_DIM), F32),
        compiler_params=_params("parallel"),
        name="compress_prompt",
    )(x, pe, w1, w2)


def _overlap(n_idx, m_idx, n_cmp, n_sel):
    d = n_idx * CMP_STRIDE - m_idx * SEL_BLOCK
    d = jnp.where(n_idx < n_cmp, d, SEL_BLOCK)
    d = jnp.where(m_idx < n_sel, d, SEL_BLOCK)
    return jnp.where(d > -CMP_BLOCK, d, SEL_BLOCK) < SEL_BLOCK


def _force_blocks(imp, m_idx, cur):
    forced = (m_idx == 0) | (m_idx == cur) | (m_idx == cur - 1)
    imp = jnp.where(forced, BIG, imp)
    return jnp.where(m_idx <= cur, imp, -BIG)


def _attend(qs, k_ref, v_ref, lo, hi, tk, mask_fn, acc_ref, m_ref, scale):
    tq = m_ref.shape[0] // NSA_GROUP
    acc_ref[...] = jnp.zeros_like(acc_ref)
    m_ref[...] = jnp.full(m_ref.shape, NEG_INF, F32)
    ones = jnp.ones((tk, LANES), BF16)

    def step(kt, c):
        k0 = pl.multiple_of(kt * tk, tk)
        kb = k_ref[pl.ds(k0, tk), :].astype(BF16)
        vb = jnp.concatenate([v_ref[pl.ds(k0, tk), :].astype(BF16), ones], axis=1)
        s = _dot_nt(qs, kb)
        mask = mask_fn(k0)
        ps, alphas = [], []
        for r in range(NSA_GROUP):
            sl = slice(r * tq, (r + 1) * tq)
            s_r = jnp.where(mask, s[sl] * scale, NEG_INF)
            m_prev = m_ref[sl, :]
            m_new = jnp.maximum(m_prev, jnp.max(s_r, axis=-1, keepdims=True))
            alphas.append(jnp.exp(m_prev - m_new))
            m_ref[sl, :] = m_new
            ps.append(jnp.where(mask, jnp.exp(s_r - _lanes(m_new, tk)), 0.0).astype(BF16))
        pv = _dot(jnp.concatenate(ps, axis=0), vb)
        for r in range(NSA_GROUP):
            sl = slice(r * tq, (r + 1) * tq)
            acc_ref[sl, :] = acc_ref[sl, :] * _lanes(alphas[r], 2 * LANES) + pv[sl]
        return c

    lax.fori_loop(lo, hi, step, 0)
    acc = acc_ref[...]
    return acc[:, :LANES] / jnp.maximum(acc[:, LANES:], 1e-30)


def _nsa_prompt_body(q_ref, kc_ref, vc_ref, ks_ref, vs_ref, kw_ref, vw_ref, gate_ref, o_ref, acc_ref, m_ref,
                     *, n_cmp, scale):
    tq = q_ref.shape[0]
    seq = ks_ref.shape[0]
    n_sel = seq // SEL_BLOCK
    qi = pl.program_id(2)
    q0 = qi * tq
    q = q_ref[...]
    qs = jnp.concatenate([q[:, r * HEAD_DIM:(r + 1) * HEAD_DIM] for r in range(NSA_GROUP)], axis=0)
    kc = kc_ref[0, 0, 0].astype(BF16)

    n_idx = lax.broadcasted_iota(I32, (LANES, tq), 0)
    qpos_l = q0 + lax.broadcasted_iota(I32, (LANES, tq), 1)
    vis_t = jnp.where(n_idx < n_cmp, n_idx * CMP_STRIDE + CMP_BLOCK - 1, seq) <= qpos_l
    s_t = _dot_nt(kc, qs)
    p_sum = jnp.zeros((LANES, tq), F32)
    for r in range(NSA_GROUP):
        s_r = jnp.where(vis_t, s_t[:, r * tq:(r + 1) * tq] * scale, NEG_INF)
        e = jnp.where(vis_t, jnp.exp(s_r - jnp.max(s_r, axis=0, keepdims=True)), 0.0)
        p_sum = p_sum + e / jnp.maximum(jnp.sum(e, axis=0, keepdims=True), 1e-30)
    m_row = lax.broadcasted_iota(I32, (LANES, LANES), 0)
    n_col = lax.broadcasted_iota(I32, (LANES, LANES), 1)
    ov_t = jnp.where(_overlap(n_col, m_row, n_cmp, n_sel), 1.0, 0.0).astype(BF16)
    imp = sum(_dot(ov_t, part) for part in _split3(p_sum))[:n_sel]
    m_idx = lax.broadcasted_iota(I32, (n_sel, tq), 0)
    cur = (q0 + lax.broadcasted_iota(I32, (n_sel, tq), 1)) // SEL_BLOCK
    imp = _force_blocks(imp, m_idx, cur)
    rank = jnp.zeros((n_sel, tq), F32)
    for mp in range(n_sel):
        other = imp[mp:mp + 1, :]
        tie = jnp.where(m_idx > mp, 1.0, 0.0)
        rank = rank + jnp.where(other > imp, 1.0, jnp.where(other == imp, tie, 0.0))
    sel = jnp.where(rank < SEL_TOP_N, 1.0, 0.0)
    sel_t = jnp.concatenate([sel, jnp.zeros((LANES - n_sel, tq), F32)], axis=0).T.astype(BF16)

    def cmp_mask(k0):
        n = lax.broadcasted_iota(I32, (tq, LANES), 1)
        qpos = q0 + lax.broadcasted_iota(I32, (tq, LANES), 0)
        return jnp.where(n < n_cmp, n * CMP_STRIDE + CMP_BLOCK - 1, seq) <= qpos

    tk_s = 4 * LANES

    def sel_mask(k0):
        blk = (k0 + lax.broadcasted_iota(I32, (LANES, tk_s), 1)) // SEL_BLOCK
        expand = jnp.where(blk == lax.broadcasted_iota(I32, (LANES, tk_s), 0), 1.0, 0.0).astype(BF16)
        chosen = _dot(sel_t, expand)
        kpos = k0 + lax.broadcasted_iota(I32, (tq, tk_s), 1)
        qpos = q0 + lax.broadcasted_iota(I32, (tq, tk_s), 0)
        return jnp.where(kpos <= qpos, chosen, 0.0) > 0.5

    def win_mask(k0):
        kpos = k0 + lax.broadcasted_iota(I32, (tq, LANES), 1)
        qpos = q0 + lax.broadcasted_iota(I32, (tq, LANES), 0)
        back = qpos - kpos
        return jnp.where(back >= 0, back, WINDOW + 1) <= WINDOW

    o_c = _attend(qs, kc_ref.at[0, 0, 0], vc_ref.at[0, 0, 0], 0, 1, LANES, cmp_mask, acc_ref, m_ref, scale)
    o_s = _attend(qs, ks_ref, vs_ref, 0, (q0 + tq + tk_s - 1) // tk_s, tk_s, sel_mask, acc_ref, m_ref, scale)
    first_w = jnp.maximum(q0 - WINDOW, 0) // LANES
    o_w = _attend(qs, kw_ref, vw_ref, first_w, qi + 1, LANES, win_mask, acc_ref, m_ref, scale)

    gate = jax.nn.sigmoid(gate_ref[...])
    for r in range(NSA_GROUP):
        sl = slice(r * tq, (r + 1) * tq)
        o = (gate[:, 3 * r:3 * r + 1] * o_c[sl] + gate[:, 3 * r + 1:3 * r + 2] * o_s[sl]
             + gate[:, 3 * r + 2:3 * r + 3] * o_w[sl])
        o_ref[:, r * HEAD_DIM:(r + 1) * HEAD_DIM] = o.astype(o_ref.dtype)


def _nsa_prompt(q, cmp, nkv, win, gates, *, batch, seq):
    tq = LANES
    assert tq == LANES and seq % (4 * LANES) == 0
    nq = seq // tq
    gw = NSA_GROUP * HEAD_DIM
    n_cmp = (seq - CMP_BLOCK) // CMP_STRIDE + 1
    rows = lambda c: pl.BlockSpec((seq, HEAD_DIM), lambda b, g, i, c=c: (b, c * N_KV_NSA + g))
    cmp_spec = lambda kind: pl.BlockSpec((1, 1, 1, LANES, HEAD_DIM), lambda b, g, i, kind=kind: (b, kind, g, 0, 0))
    return pl.pallas_call(
        functools.partial(_nsa_prompt_body, n_cmp=n_cmp, scale=HEAD_DIM ** -0.5),
        grid=(batch, N_KV_NSA, nq),
        in_specs=[pl.BlockSpec((tq, gw), lambda b, g, i: (b * nq + i, g)),
                  cmp_spec(0), cmp_spec(1), rows(2), rows(3), rows(0), rows(1),
                  pl.BlockSpec((tq, LANES), lambda b, g, i: (b * nq + i, g))],
        out_specs=pl.BlockSpec((tq, gw), lambda b, g, i: (b * nq + i, g)),
        out_shape=jax.ShapeDtypeStruct(q.shape, BF16),
        scratch_shapes=[pltpu.VMEM((NSA_GROUP * tq, 2 * LANES), F32), pltpu.VMEM((NSA_GROUP * tq, LANES), F32)],
        compiler_params=_params("parallel", "parallel", "arbitrary"),
        name="nsa_prompt",
    )(q, cmp, cmp, nkv, nkv, win, win, gates)


def _softmax_rows(s, vis):
    s = jnp.where(vis, s, NEG_INF)
    e = jnp.where(vis, jnp.exp(s - jnp.max(s, axis=-1, keepdims=True)), 0.0)
    return e / jnp.maximum(jnp.sum(e, axis=-1, keepdims=True), 1e-30)


def _nsa_sample_body(pt_ref, *refs, n_pages, past, scale):
    del pt_ref
    pages = refs[:n_pages]
    new_ref, wst_ref, wnew_ref, q_ref, gate_ref, pe_ref, w1_ref, w2_ref, o_ref = refs[n_pages:]
    slabs = 4 * N_KV_NSA
    page = pages[0].shape[1] // slabs
    pos = past
    n_keys = past + 1
    n_cmp = (n_keys - CMP_BLOCK) // CMP_STRIDE + 1
    n_sel = -(-n_keys // SEL_BLOCK)
    w_buf = wst_ref.shape[1]
    gd = N_KV_NSA * HEAD_DIM
    r8 = NSA_GROUP

    cmp = []
    for kind in range(2):
        xs = []
        for g in range(N_KV_NSA):
            slab = kind * N_KV_NSA + g
            cols = []
            for l in range(CMP_STRIDE):
                cols.append(jnp.concatenate(
                    [pg[0, pl.ds(l * slabs + slab, page // CMP_STRIDE, stride=CMP_STRIDE * slabs), :] for pg in pages],
                    axis=0))
            xs.append(jnp.concatenate(cols, axis=1))
        cmp.append(_compress_rows(jnp.concatenate(xs, axis=0), pe_ref, w1_ref, w2_ref, kind, n_cmp))

    lane = lax.broadcasted_iota(I32, (r8, LANES), 1)
    sq_row = lax.broadcasted_iota(I32, (LANES, LANES), 0)
    sq_col = lax.broadcasted_iota(I32, (LANES, LANES), 1)
    ov = jnp.where(_overlap(sq_row, sq_col, n_cmp, n_sel), 1.0, 0.0).astype(BF16)
    lower_first = jnp.where(sq_row < sq_col, 1.0, 0.0)
    n_all = past + LANES
    tok = lax.broadcasted_iota(I32, (r8, n_all), 1)
    expand = jnp.where(lax.broadcasted_iota(I32, (LANES, n_all), 1) // SEL_BLOCK
                       == lax.broadcasted_iota(I32, (LANES, n_all), 0), 1.0, 0.0).astype(BF16)
    widx = lax.broadcasted_iota(I32, (r8, w_buf + LANES), 1)
    gate = jax.nn.sigmoid(gate_ref[0])

    def with_new(past_rows, new_row):
        return jnp.concatenate([past_rows, jnp.broadcast_to(new_row, (LANES, HEAD_DIM))], axis=0).astype(BF16)

    for g in range(N_KV_NSA):
        q = q_ref[0, g * r8:(g + 1) * r8, :].astype(BF16)
        kc = cmp[0][g * LANES:(g + 1) * LANES].astype(BF16)
        vc = cmp[1][g * LANES:(g + 1) * LANES].astype(BF16)
        vis_c = jnp.where(lane < n_cmp, lane * CMP_STRIDE + CMP_BLOCK - 1, pos + 1) <= pos
        p_c = _softmax_rows(_dot_nt(q, kc) * scale, vis_c)
        o_c = _dot(p_c.astype(BF16), vc)

        p_sum = jnp.broadcast_to(jnp.sum(p_c, axis=0, keepdims=True), (r8, LANES))
        imp = sum(_dot(part, ov) for part in _split3(p_sum))
        imp = _force_blocks(imp, lane, pos // SEL_BLOCK)
        imp_b = jnp.broadcast_to(imp[0:1], (LANES, LANES))
        imp_c = imp_b.T
        beats = jnp.where(imp_c > imp_b, 1.0, jnp.where(imp_c == imp_b, lower_first, 0.0))
        rank = jnp.sum(beats, axis=0, keepdims=True)
        sel = jnp.broadcast_to(jnp.where(rank < SEL_TOP_N, 1.0, 0.0), (r8, LANES)).astype(BF16)
        vis_s = jnp.where(tok <= pos, _dot(sel, expand), 0.0) > 0.5

        c_k = (2 * N_KV_NSA + g) * HEAD_DIM
        c_v = (3 * N_KV_NSA + g) * HEAD_DIM
        past_rows = lambda slab: jnp.concatenate([pg[0, pl.ds(slab, page, stride=slabs), :] for pg in pages], axis=0)
        k_s = with_new(past_rows(2 * N_KV_NSA + g), new_ref[0, :, c_k:c_k + HEAD_DIM])
        v_s = with_new(past_rows(3 * N_KV_NSA + g), new_ref[0, :, c_v:c_v + HEAD_DIM])
        p_s = _softmax_rows(_dot_nt(q, k_s) * scale, vis_s)
        o_s = _dot(p_s.astype(BF16), v_s)

        k_w = with_new(wst_ref[0, :, g * HEAD_DIM:(g + 1) * HEAD_DIM], wnew_ref[0, :, g * HEAD_DIM:(g + 1) * HEAD_DIM])
        v_w = with_new(wst_ref[0, :, gd + g * HEAD_DIM:gd + (g + 1) * HEAD_DIM],
                       wnew_ref[0, :, gd + g * HEAD_DIM:gd + (g + 1) * HEAD_DIM])
        p_w = _softmax_rows(_dot_nt(q, k_w) * scale, widx <= w_buf)
        o_w = _dot(p_w.astype(BF16), v_w)

        gt = gate[g * r8:(g + 1) * r8]
        o_ref[0, g * r8:(g + 1) * r8, :] = gt[:, 0:1] * o_c + gt[:, 1:2] * o_s + gt[:, 2:3] * o_w


def _nsa_sample(q, gates, new_nkv, win_state, new_win, cache, page_table, pe, w1, w2):
    s, n_pages = page_table.shape
    page_rows = cache.shape[1]
    past = n_pages * page_rows // (4 * N_KV_NSA)
    assert past // CMP_STRIDE == LANES and win_state.shape[1] <= WINDOW
    page_spec = lambda p: pl.BlockSpec((1, page_rows, HEAD_DIM), lambda b, pt, p=p: (pt[b, p], 0, 0))
    per_seq = lambda a: pl.BlockSpec((1,) + a.shape[1:], lambda b, pt: (b, 0, 0))
    whole = lambda a: pl.BlockSpec(a.shape, lambda b, pt, n=a.ndim: (0,) * n)
    grid_spec = pltpu.PrefetchScalarGridSpec(
        num_scalar_prefetch=1,
        grid=(s,),
        in_specs=[page_spec(p) for p in range(n_pages)]
        + [per_seq(new_nkv), per_seq(win_state), per_seq(new_win), per_seq(q), per_seq(gates),
           whole(pe), whole(w1), whole(w2)],
        out_specs=pl.BlockSpec((1, N_HEADS_NSA, HEAD_DIM), lambda b, pt: (b, 0, 0)),
    )
    return pl.pallas_call(
        functools.partial(_nsa_sample_body, n_pages=n_pages, past=past, scale=HEAD_DIM ** -0.5),
        grid_spec=grid_spec,
        out_shape=jax.ShapeDtypeStruct((s, N_HEADS_NSA, HEAD_DIM), F32),
        compiler_params=_params("parallel"),
        name="nsa_sample",
    )(page_table, *([cache] * n_pages), new_nkv, win_state, new_win, q, gates, pe, w1, w2)


def _rope_tables(pos):
    half = HEAD_DIM // 2
    inv_freq = ROPE_THETA ** (-jnp.arange(half, dtype=F32) / half)
    ang = pos.astype(F32)[:, None] * inv_freq[None, :]
    cos, sin = jnp.cos(ang), jnp.sin(ang)
    return jnp.concatenate([cos, cos], axis=1), jnp.concatenate([-sin, sin], axis=1)


def _prepare_weights(w_in, cmp_k_pe, cmp_k_w1, cmp_k_w2, cmp_v_pe, cmp_v_w1, cmp_v_w2,
                     w_branch_sb, w_branch_nsa, w_out, w_gate, w_up, w_down):
    d_sb = N_HEADS_SB * HEAD_DIM
    d_nsa = N_HEADS_NSA * HEAD_DIM
    d_kv = N_KV_NSA * HEAD_DIM
    cuts = [0, d_sb, 3 * d_sb, 3 * d_sb + d_nsa, 3 * d_sb + d_nsa + 4 * d_kv, 3 * d_sb + d_nsa + 6 * d_kv]
    seg = lambda i: w_in[:, cuts[i]:cuts[i + 1]].astype(BF16)
    n_gate = 3 * N_HEADS_NSA
    w_ng = w_in[:, cuts[5]:cuts[5] + n_gate].reshape(-1, N_KV_NSA, 3 * NSA_GROUP)
    w_ng = jnp.pad(w_ng, ((0, 0), (0, 0), (0, LANES - 3 * NSA_GROUP))).reshape(-1, N_KV_NSA * LANES)
    d_ff = w_gate.shape[1]
    ff_pad = -d_ff % FF_PAD_MULTIPLE
    half = CMP_STRIDE * HEAD_DIM
    halves = lambda w1: w1.reshape(2, half, -1)
    return dict(
        sb_q=seg(0), sb_kv=seg(1), nsa_q=seg(2), nsa_kv=seg(3), nsa_win=seg(4),
        nsa_gate=w_ng.astype(BF16), merge=w_in[:, cuts[5] + n_gate:].astype(BF16),
        pe=jnp.stack([cmp_k_pe.reshape(2, 1, half), cmp_v_pe.reshape(2, 1, half)]),
        w1=jnp.stack([halves(cmp_k_w1), halves(cmp_v_w1)]).astype(BF16),
        w2=jnp.stack([cmp_k_w2, cmp_v_w2]).astype(BF16),
        branch_sb=w_branch_sb.astype(BF16), branch_nsa=w_branch_nsa.astype(BF16), out=w_out.astype(BF16),
        gate=jnp.pad(w_gate, ((0, 0), (0, ff_pad))).astype(BF16), up=jnp.pad(w_up, ((0, 0), (0, ff_pad))).astype(BF16),
        down=jnp.pad(w_down, ((0, ff_pad), (0, 0))).astype(BF16),
    )


def _project(x, g_attn, w, cos, sin, tm, q_dtype):
    h = _rmsnorm(x, g_attn, BF16, min(tm, 256))
    mm = functools.partial(_matmul, h, tm=tm)
    rope = functools.partial(_matmul_rope, h, cos=cos, sin=sin, tm=tm, tn=4 * LANES)
    return dict(
        sb_q=mm(w["sb_q"], q_dtype, tn=512, name="proj_sb_q"),
        sb_kv=mm(w["sb_kv"], F32, tn=512, name="proj_sb_kv"),
        nsa_q=rope(w=w["nsa_q"], out_dtype=q_dtype, rope_slabs=(True,) * 4, name="proj_nsa_q"),
        nsa_kv=rope(w=w["nsa_kv"], out_dtype=F32, rope_slabs=(True, True, False, False), name="proj_nsa_kv"),
        nsa_win=rope(w=w["nsa_win"], out_dtype=F32, rope_slabs=(True, True, False, False), name="proj_nsa_win"),
        nsa_gate=mm(w["nsa_gate"], F32, tn=256, name="proj_nsa_gate"),
        merge=mm(w["merge"], F32, tn=512, name="proj_merge"),
    )


def _finish(x, o_sb, o_nsa, merge, w, g_ffn, g_final, tm):
    merged = _matmul_merge(o_sb, w["branch_sb"], o_nsa, w["branch_nsa"], merge, tm=tm, tn=512, name="branch_merge")
    x = _matmul_residual(merged, w["out"], x, tm=tm, tn=512, tk=merged.shape[1], name="out_proj")
    h = _rmsnorm(x, g_ffn, BF16, min(tm, 256))
    ff = _matmul_gate_up(h, w["gate"], w["up"], tm=tm, tn=512, name="ffn_gate_up")
    x = _matmul_residual(ff, w["down"], x, tm=tm, tn=1024, tk=ff.shape[1] // 8, name="ffn_down")
    return _rmsnorm(x, g_final, F32, min(tm, 256))


def kernel(x_prompt, x_sample, cache_sb_kv, cache_nsa_kv, state_nsa_win, page_table, g_attn, w_in, cmp_k_pe, cmp_k_w1, cmp_k_w2, cmp_v_pe, cmp_v_w1, cmp_v_w2, w_branch_sb, w_branch_nsa, w_out, g_ffn, w_gate, w_up, w_down, g_final):
    depth = w_in.shape[0]
    assert depth == 1
    batch, seq, d_model = x_prompt.shape
    n_dec = x_sample.shape[0]
    assert x_sample.shape[1] == 1
    n_pages = page_table.shape[1]
    page = cache_sb_kv.shape[2]
    past = n_pages * page
    w_buf = state_nsa_win.shape[2]
    win_len = min(WINDOW, seq)
    layer = 0
    w = _prepare_weights(w_in[layer], cmp_k_pe[layer], cmp_k_w1[layer], cmp_k_w2[layer], cmp_v_pe[layer],
                         cmp_v_w1[layer], cmp_v_w2[layer], w_branch_sb[layer], w_branch_nsa[layer], w_out[layer],
                         w_gate[layer], w_up[layer], w_down[layer])

    xp = x_prompt.reshape(batch * seq, d_model)
    cos_p, sin_p = _rope_tables(jnp.arange(seq, dtype=I32))
    pp = _project(xp, g_attn[layer], w, cos_p, sin_p, 1024, BF16)
    o_sb = _sb_prompt(pp["sb_q"], pp["sb_kv"], batch=batch, seq=seq)
    cmp_p = _compress_prompt(pp["nsa_kv"], w["pe"], w["w1"], w["w2"], batch=batch, seq=seq)
    o_nsa = _nsa_prompt(pp["nsa_q"], cmp_p, pp["nsa_kv"], pp["nsa_win"], pp["nsa_gate"], batch=batch, seq=seq)
    y_prompt = _finish(xp, o_sb, o_nsa, pp["merge"], w, g_ffn[layer], g_final, 1024)

    xs = x_sample.reshape(n_dec, d_model)
    cos_s, sin_s = _rope_tables(jnp.full((n_dec,), past, dtype=I32))
    ps = _project(xs, g_attn[layer], w, cos_s, sin_s, n_dec, F32)
    sb_cache = cache_sb_kv[layer].reshape(cache_sb_kv.shape[1], page, -1)
    o_sb_s = _sb_sample(ps["sb_q"].reshape(n_dec, 1, -1), sb_cache, page_table)
    nsa_cache = cache_nsa_kv[layer].reshape(cache_nsa_kv.shape[1], -1, HEAD_DIM)
    gate_s = ps["nsa_gate"].reshape(n_dec, N_KV_NSA, LANES)[:, :, :3 * NSA_GROUP].reshape(n_dec, N_HEADS_NSA, 3)
    gate_s = jnp.pad(gate_s, ((0, 0), (0, 0), (0, LANES - 3)))
    win_state = state_nsa_win[layer].reshape(n_dec, w_buf, -1)
    o_nsa_s = _nsa_sample(ps["nsa_q"].reshape(n_dec, N_HEADS_NSA, HEAD_DIM), gate_s,
                          ps["nsa_kv"].reshape(n_dec, 1, -1), win_state, ps["nsa_win"].reshape(n_dec, 1, -1),
                          nsa_cache, page_table, w["pe"], w["w1"], w["w2"])
    y_sample = _finish(xs, o_sb_s.reshape(n_dec, -1).astype(BF16), o_nsa_s.reshape(n_dec, -1).astype(BF16),
                       ps["merge"], w, g_ffn[layer], g_final, n_dec)

    new_win_s = jnp.concatenate([win_state, ps["nsa_win"].reshape(n_dec, 1, -1)], axis=1)[:, 1:]
    return (
        y_prompt.reshape(batch, seq, d_model),
        y_sample.reshape(n_dec, 1, d_model),
        pp["sb_kv"].reshape(depth, batch, seq, 2, N_HEADS_SB, HEAD_DIM),
        ps["sb_kv"].reshape(depth, n_dec, 1, 2, N_HEADS_SB, HEAD_DIM),
        pp["nsa_kv"].reshape(depth, batch, seq, 4, N_KV_NSA, HEAD_DIM),
        ps["nsa_kv"].reshape(depth, n_dec, 1, 4, N_KV_NSA, HEAD_DIM),
        pp["nsa_win"].reshape(batch, seq, -1)[:, seq - win_len:].reshape(depth, batch, win_len, 2, N_KV_NSA, HEAD_DIM),
        new_win_s.reshape(depth, n_dec, w_buf, 2, N_KV_NSA, HEAD_DIM),
    )
```

```python
import functools

import jax
import jax.numpy as jnp
from jax import lax
from jax.experimental import pallas as pl
from jax.experimental.pallas import tpu as pltpu

F32 = jnp.float32
BF16 = jnp.bfloat16
I32 = jnp.int32

HEAD_DIM = 128
N_HEADS_SB = 16
N_HEADS_NSA = 16
N_KV_NSA = 2
NSA_GROUP = N_HEADS_NSA // N_KV_NSA
CMP_BLOCK = 32
CMP_STRIDE = 16
SEL_BLOCK = 64
SEL_TOP_N = 16
WINDOW = 512
ROPE_THETA = 10000.0
RMS_EPS = 1e-6
NEG_INF = -1e30
BIG = 1e30
LANES = 128
VMEM_LIMIT_BYTES = 48 * 2**20
FF_PAD_MULTIPLE = 1024


def _params(*sem):
    return pltpu.CompilerParams(dimension_semantics=sem, vmem_limit_bytes=VMEM_LIMIT_BYTES)


def _dot(a, b):
    return jnp.dot(a, b, preferred_element_type=F32)


def _dot_nt(a, b):
    return lax.dot_general(a, b, (((1,), (1,)), ((), ())), preferred_element_type=F32)


def _lanes(x, n):
    return x if n == LANES else jnp.concatenate([x] * (n // LANES), axis=1)


def _split2(x):
    hi = x.astype(BF16)
    lo = (x - hi.astype(F32)).astype(BF16)
    return hi, lo


def _split3(x):
    hi = x.astype(BF16)
    r1 = x - hi.astype(F32)
    mid = r1.astype(BF16)
    lo = (r1 - mid.astype(F32)).astype(BF16)
    return hi, mid, lo


def _rmsnorm_body(x_ref, g_ref, o_ref):
    x = x_ref[...]
    inv = lax.rsqrt(jnp.mean(x * x, axis=-1, keepdims=True) + RMS_EPS)
    o_ref[...] = (x * inv * g_ref[...]).astype(o_ref.dtype)


def _rmsnorm(x, g, out_dtype, tm):
    m, d = x.shape
    return pl.pallas_call(
        _rmsnorm_body,
        grid=(m // tm,),
        in_specs=[pl.BlockSpec((tm, d), lambda i: (i, 0)), pl.BlockSpec((1, d), lambda i: (0, 0))],
        out_specs=pl.BlockSpec((tm, d), lambda i: (i, 0)),
        out_shape=jax.ShapeDtypeStruct((m, d), out_dtype),
        compiler_params=_params("arbitrary"),
        name="rmsnorm",
    )(x, g.reshape(1, d))


def _rope_tile(y, cos, sin, rope_slabs):
    out = []
    for c, roped in enumerate(rope_slabs):
        x = y[:, c * LANES:(c + 1) * LANES]
        out.append(x * cos + pltpu.roll(x, HEAD_DIM // 2, 1) * sin if roped else x)
    return out[0] if len(out) == 1 else jnp.concatenate(out, axis=1)


def _mm_body(a_ref, w_ref, o_ref):
    o_ref[...] = _dot(a_ref[...], w_ref[...]).astype(o_ref.dtype)


def _mm_rope_body(a_ref, w_ref, cos_ref, sin_ref, o_ref, *, rope_slabs):
    y = _dot(a_ref[...], w_ref[...])
    o_ref[...] = _rope_tile(y, cos_ref[...], sin_ref[...], rope_slabs).astype(o_ref.dtype)


def _mm_res_body(a_ref, w_ref, r_ref, o_ref):
    @pl.when(pl.program_id(2) == 0)
    def _():
        o_ref[...] = r_ref[...]

    o_ref[...] += _dot(a_ref[...], w_ref[...])


def _mm_gate_up_body(a_ref, wg_ref, wu_ref, o_ref):
    a = a_ref[...]
    o_ref[...] = (jax.nn.silu(_dot(a, wg_ref[...])) * _dot(a, wu_ref[...])).astype(o_ref.dtype)


def _mm_merge_body(a1_ref, w1_ref, a2_ref, w2_ref, g1_ref, g2_ref, o_ref):
    y1 = _dot(a1_ref[...], w1_ref[...])
    y2 = _dot(a2_ref[...], w2_ref[...])
    o_ref[...] = (jax.nn.sigmoid(g1_ref[...]) * y1 + jax.nn.sigmoid(g2_ref[...]) * y2).astype(o_ref.dtype)


def _tile(n, pref):
    return pref if n % pref == 0 else n


def _matmul(a, w, out_dtype, *, tm, tn, name, cols=None):
    m, k = a.shape
    c0, c1 = cols or (0, w.shape[1])
    n = c1 - c0
    tm, tn = _tile(m, tm), _tile(n, tn)
    j0 = c0 // tn
    assert j0 * tn == c0
    return pl.pallas_call(
        _mm_body,
        grid=(m // tm, n // tn),
        in_specs=[pl.BlockSpec((tm, k), lambda i, j: (i, 0)), pl.BlockSpec((k, tn), lambda i, j: (0, j0 + j))],
        out_specs=pl.BlockSpec((tm, tn), lambda i, j: (i, j)),
        out_shape=jax.ShapeDtypeStruct((m, n), out_dtype),
        compiler_params=_params("arbitrary", "arbitrary"),
        name=name,
    )(a, w)


def _matmul_rope(a, w, cos, sin, out_dtype, *, tm, tn, rope_slabs, name, cols):
    m, k = a.shape
    c0, c1 = cols
    n = c1 - c0
    tm, tn = _tile(m, tm), _tile(n, tn)
    j0 = c0 // tn
    assert j0 * tn == c0
    pos_tiles = cos.shape[0] // tm
    return pl.pallas_call(
        functools.partial(_mm_rope_body, rope_slabs=rope_slabs),
        grid=(m // tm, n // tn),
        in_specs=[pl.BlockSpec((tm, k), lambda i, j: (i, 0)), pl.BlockSpec((k, tn), lambda i, j: (0, j0 + j)),
                  pl.BlockSpec((tm, LANES), lambda i, j: (i % pos_tiles, 0)),
                  pl.BlockSpec((tm, LANES), lambda i, j: (i % pos_tiles, 0))],
        out_specs=pl.BlockSpec((tm, tn), lambda i, j: (i, j)),
        out_shape=jax.ShapeDtypeStruct((m, n), out_dtype),
        compiler_params=_params("arbitrary", "arbitrary"),
        name=name,
    )(a, w, cos, sin)


def _matmul_residual(a, w, res, *, tm, tn, tk, name):
    m, k = a.shape
    n = w.shape[1]
    tm, tn, tk = _tile(m, tm), _tile(n, tn), _tile(k, tk)
    return pl.pallas_call(
        _mm_res_body,
        grid=(m // tm, n // tn, k // tk),
        in_specs=[pl.BlockSpec((tm, tk), lambda i, j, kk: (i, kk)), pl.BlockSpec((tk, tn), lambda i, j, kk: (kk, j)),
                  pl.BlockSpec((tm, tn), lambda i, j, kk: (i, j))],
        out_specs=pl.BlockSpec((tm, tn), lambda i, j, kk: (i, j)),
        out_shape=jax.ShapeDtypeStruct((m, n), F32),
        compiler_params=_params("arbitrary", "arbitrary", "arbitrary"),
        name=name,
    )(a, w, res)


def _matmul_gate_up(a, wg, wu, *, tm, tn, name):
    m, k = a.shape
    n = wg.shape[1]
    tm, tn = _tile(m, tm), _tile(n, tn)
    return pl.pallas_call(
        _mm_gate_up_body,
        grid=(m // tm, n // tn),
        in_specs=[pl.BlockSpec((tm, k), lambda i, j: (i, 0)), pl.BlockSpec((k, tn), lambda i, j: (0, j)),
                  pl.BlockSpec((k, tn), lambda i, j: (0, j))],
        out_specs=pl.BlockSpec((tm, tn), lambda i, j: (i, j)),
        out_shape=jax.ShapeDtypeStruct((m, n), BF16),
        compiler_params=_params("arbitrary", "arbitrary"),
        name=name,
    )(a, wg, wu)


def _matmul_merge(a1, w1, a2, w2, gates, *, tm, tn, name):
    m, k = a1.shape
    n = w1.shape[1]
    tm, tn = _tile(m, tm), _tile(n, tn)
    nj = n // tn
    return pl.pallas_call(
        _mm_merge_body,
        grid=(m // tm, nj),
        in_specs=[pl.BlockSpec((tm, k), lambda i, j: (i, 0)), pl.BlockSpec((k, tn), lambda i, j: (0, j)),
                  pl.BlockSpec((tm, k), lambda i, j: (i, 0)), pl.BlockSpec((k, tn), lambda i, j: (0, j)),
                  pl.BlockSpec((tm, tn), lambda i, j: (i, j)), pl.BlockSpec((tm, tn), lambda i, j: (i, j + nj))],
        out_specs=pl.BlockSpec((tm, tn), lambda i, j: (i, j)),
        out_shape=jax.ShapeDtypeStruct((m, n), BF16),
        compiler_params=_params("arbitrary", "arbitrary"),
        name=name,
    )(a1, w1, a2, w2, gates, gates)


def _suffix_matrix(t):
    row = lax.broadcasted_iota(I32, (t, 2 * t), 0)
    col = lax.broadcasted_iota(I32, (t, 2 * t), 1)
    return jnp.where((row > col) | (col >= t), 1.0, 0.0).astype(BF16)


def _log_one_minus_beta(z):
    return -(jnp.maximum(z, 0.0) + jnp.log1p(jnp.exp(-jnp.abs(z))))


def _sb_block(z, v, u, carry, causal):
    sub = LANES
    n_sub = z.shape[1] // sub
    l1m = _log_one_minus_beta(z)
    if causal is not None:
        l1m = jnp.where(causal, l1m, 0.0)
    hi, lo = _split2(l1m)
    tails = [None] * n_sub
    for j in reversed(range(n_sub)):
        sl = slice(j * sub, (j + 1) * sub)
        sums = _dot(hi[:, sl], u) + _dot(lo[:, sl], u)
        tails[j] = sums[:, :sub] + carry
        carry = carry + sums[:, sub:]
    tail = tails[0] if n_sub == 1 else jnp.concatenate(tails, axis=1)
    a = jnp.exp(l1m + z + tail)
    if causal is not None:
        a = jnp.where(causal, a, 0.0)
    return _dot(a.astype(BF16), v), carry


def _sb_prompt_body(q_ref, k_ref, v_ref, o_ref, *, tq, tk, scale):
    seq = q_ref.shape[0]
    u = _suffix_matrix(LANES)
    col_minus_row = lax.broadcasted_iota(I32, (tq, tk), 1) - lax.broadcasted_iota(I32, (tq, tk), 0)

    def load_kv(kb):
        k0 = pl.multiple_of(kb * tk, tk)
        return k_ref[pl.ds(k0, tk), :].astype(BF16), v_ref[pl.ds(k0, tk), :].astype(BF16)

    def q_tile(qi, _):
        q0 = pl.multiple_of(qi * tq, tq)
        q = q_ref[pl.ds(q0, tq), :]
        kb_diag = q0 // tk
        k, v = load_kv(kb_diag)
        causal = col_minus_row < q0 - kb_diag * tk
        acc, carry = _sb_block(_dot_nt(q, k) * scale, v, u, jnp.zeros((tq, LANES), F32), causal)

        def left(t, c):
            k, v = load_kv(kb_diag - 1 - t)
            pv, carry = _sb_block(_dot_nt(q, k) * scale, v, u, c[1], None)
            return c[0] + pv, carry

        acc, _ = lax.fori_loop(0, kb_diag, left, (acc, carry))
        o_ref[pl.ds(q0, tq), :] = acc.astype(o_ref.dtype)
        return 0

    lax.fori_loop(0, seq // tq, q_tile, 0)


def _sb_prompt(q, kv, *, batch, seq):
    h = N_HEADS_SB
    tq, tk = 2 * LANES, 4 * LANES
    assert tk % tq == 0 and seq % tk == 0
    return pl.pallas_call(
        functools.partial(_sb_prompt_body, tq=tq, tk=tk, scale=HEAD_DIM ** -0.5),
        grid=(batch, h),
        in_specs=[pl.BlockSpec((seq, HEAD_DIM), lambda b, i: (b, i)),
                  pl.BlockSpec((seq, HEAD_DIM), lambda b, i: (b, i)),
                  pl.BlockSpec((seq, HEAD_DIM), lambda b, i: (b, h + i))],
        out_specs=pl.BlockSpec((seq, HEAD_DIM), lambda b, i: (b, i)),
        out_shape=jax.ShapeDtypeStruct(q.shape, BF16),
        compiler_params=_params("arbitrary", "arbitrary"),
        name="sb_prompt",
    )(q, kv, kv)


SB_PAGES_PER_STEP = 8
SUBLANES = 8


def _sb_sample_body(pt_ref, q_ref, *refs, scale):
    del pt_ref
    blocks, (o_ref, acc_ref, carry_ref) = refs[:-3], refs[-3:]
    pp = pl.program_id(1)
    d = N_HEADS_SB * HEAD_DIM

    @pl.when(pp == 0)
    def _():
        acc_ref[...] = jnp.zeros_like(acc_ref)
        carry_ref[...] = jnp.zeros_like(carry_ref)

    head = lax.broadcasted_iota(I32, (N_HEADS_SB, d), 0)
    lane_head = lax.broadcasted_iota(I32, (N_HEADS_SB, d), 1) // HEAD_DIM
    own = head == lane_head
    q_bd = jnp.where(own, jnp.broadcast_to(q_ref[0], (N_HEADS_SB, d)), 0.0).astype(BF16)
    u = _suffix_matrix(LANES)

    def heads_to_lanes(parts):
        t = parts[0].shape[0]
        return jnp.concatenate([p.reshape(t * SUBLANES, HEAD_DIM)[pl.ds(h, t, stride=SUBLANES), :]
                                for p in parts for h in range(SUBLANES)], axis=1).astype(BF16)

    per_page = 2 * N_HEADS_SB // SUBLANES
    acc, carry = acc_ref[...], carry_ref[...]
    for j in range(len(blocks) // per_page):
        parts = blocks[j * per_page:(j + 1) * per_page]
        k = heads_to_lanes(parts[:per_page // 2])
        v = heads_to_lanes(parts[per_page // 2:])
        pv, carry = _sb_block(_dot_nt(q_bd, k) * scale, v, u, carry, None)
        acc = acc + pv
    acc_ref[...] = acc
    carry_ref[...] = carry

    @pl.when(pp == pl.num_programs(1) - 1)
    def _():
        o_ref[0] = jnp.sum(jnp.where(own, acc, 0.0), axis=0, keepdims=True)


def _sb_sample(q, cache, page_table):
    s, n_pages = page_table.shape
    d = N_HEADS_SB * HEAD_DIM
    page = cache.shape[2]
    assert page == LANES and n_pages % SB_PAGES_PER_STEP == 0
    steps = n_pages // SB_PAGES_PER_STEP

    def block(j, kv, half):
        return pl.BlockSpec((None, None, page, None, SUBLANES, HEAD_DIM),
                            lambda b, p, pt: (0, pt[b, n_pages - 1 - (p * SB_PAGES_PER_STEP + j)], 0, kv, half, 0))

    grid_spec = pltpu.PrefetchScalarGridSpec(
        num_scalar_prefetch=1,
        grid=(s, steps),
        in_specs=[pl.BlockSpec((1, 1, d), lambda b, p, pt: (b, 0, 0))]
        + [block(j, kv, half) for j in range(SB_PAGES_PER_STEP) for kv in range(2)
           for half in range(N_HEADS_SB // SUBLANES)],
        out_specs=pl.BlockSpec((1, 1, d), lambda b, p, pt: (b, 0, 0)),
        scratch_shapes=[pltpu.VMEM((N_HEADS_SB, d), F32), pltpu.VMEM((N_HEADS_SB, LANES), F32)],
    )
    n_blocks = SB_PAGES_PER_STEP * 2 * N_HEADS_SB // SUBLANES
    return pl.pallas_call(
        functools.partial(_sb_sample_body, scale=HEAD_DIM ** -0.5),
        grid_spec=grid_spec,
        out_shape=jax.ShapeDtypeStruct((s, 1, d), F32),
        compiler_params=_params("arbitrary", "arbitrary"),
        name="sb_sample",
    )(page_table, q, *([cache] * n_blocks))


def _compress_rows(x, pe_ref, w1_ref, w2_ref, kind, n_cmp):
    c = LANES
    pa = _dot((x + pe_ref[kind, 0]).astype(BF16), w1_ref[kind, 0])
    pb = _dot((x + pe_ref[kind, 1]).astype(BF16), w1_ref[kind, 1])
    pb = jnp.concatenate([pltpu.roll(pb[i:i + c], c - 1, 0) for i in range(0, x.shape[0], c)], axis=0)
    out = _dot(jax.nn.gelu(pa + pb).astype(BF16), w2_ref[kind])
    row = lax.broadcasted_iota(I32, out.shape, 0) % c
    return jnp.where(row < n_cmp, out, 0.0)


def _compress_prompt_body(*refs, n_cmp):
    rows, (pe_ref, w1_ref, w2_ref, o_ref) = refs[:2 * N_KV_NSA], refs[2 * N_KV_NSA:]
    chunks = rows[0].shape[0] // CMP_STRIDE
    for kind in range(2):
        xs = [jnp.concatenate([rows[kind * N_KV_NSA + g][pl.ds(l, chunks, stride=CMP_STRIDE), :]
                               for l in range(CMP_STRIDE)], axis=1) for g in range(N_KV_NSA)]
        out = _compress_rows(jnp.concatenate(xs, axis=0), pe_ref, w1_ref, w2_ref, kind, n_cmp)
        for g in range(N_KV_NSA):
            o_ref[0, kind, g] = out[g * LANES:(g + 1) * LANES]


def _compress_prompt(nkv, pe, w1, w2, *, batch, seq):
    chunks = seq // CMP_STRIDE
    assert chunks == LANES
    n_cmp = (seq - CMP_BLOCK) // CMP_STRIDE + 1
    col = lambda c: pl.BlockSpec((seq, HEAD_DIM), lambda b, c=c: (b, c))
    return pl.pallas_call(
        functools.partial(_compress_prompt_body, n_cmp=n_cmp),
        grid=(batch,),
        in_specs=[col(c) for c in range(2 * N_KV_NSA)]
        + [pl.BlockSpec(pe.shape, lambda b: (0, 0, 0, 0)),
           pl.BlockSpec(w1.shape, lambda b: (0, 0, 0, 0)),
           pl.BlockSpec(w2.shape, lambda b: (0, 0, 0))],
        out_specs=pl.BlockSpec((1, 2, N_KV_NSA, chunks, HEAD_DIM), lambda b: (b, 0, 0, 0, 0)),
        out_shape=jax.ShapeDtypeStruct((batch, 2, N_KV_NSA, chunks, HEAD_DIM), F32),
        compiler_params=_params("arbitrary"),
        name="compress_prompt",
    )(*([nkv] * (2 * N_KV_NSA)), pe, w1, w2)


def _overlap(n_idx, m_idx, n_cmp, n_sel):
    d = n_idx * CMP_STRIDE - m_idx * SEL_BLOCK
    d = jnp.where(n_idx < n_cmp, d, SEL_BLOCK)
    d = jnp.where(m_idx < n_sel, d, SEL_BLOCK)
    return jnp.where(d > -CMP_BLOCK, d, SEL_BLOCK) < SEL_BLOCK


def _force_blocks(imp, m_idx, cur):
    forced = (m_idx == 0) | (m_idx == cur) | (m_idx == cur - 1)
    imp = jnp.where(forced, BIG, imp)
    return jnp.where(m_idx <= cur, imp, -BIG)


def _attend(qs, k_ref, v_ref, lo, hi, tk, mask_fn, acc_ref, m_ref, scale):
    tq = m_ref.shape[0] // NSA_GROUP
    acc_ref[...] = jnp.zeros_like(acc_ref)
    m_ref[...] = jnp.full(m_ref.shape, NEG_INF, F32)
    ones = jnp.ones((tk, LANES), BF16)

    def step(kt, c):
        k0 = pl.multiple_of(kt * tk, tk)
        kb = k_ref[pl.ds(k0, tk), :].astype(BF16)
        vb = jnp.concatenate([v_ref[pl.ds(k0, tk), :].astype(BF16), ones], axis=1)
        s = _dot_nt(qs, kb)
        mask = mask_fn(k0)
        ps, alphas = [], []
        for r in range(NSA_GROUP):
            sl = slice(r * tq, (r + 1) * tq)
            s_r = jnp.where(mask, s[sl] * scale, NEG_INF)
            m_prev = m_ref[sl, :]
            m_new = jnp.maximum(m_prev, jnp.max(s_r, axis=-1, keepdims=True))
            alphas.append(jnp.exp(m_prev - m_new))
            m_ref[sl, :] = m_new
            ps.append(jnp.where(mask, jnp.exp(s_r - _lanes(m_new, tk)), 0.0).astype(BF16))
        pv = _dot(jnp.concatenate(ps, axis=0), vb)
        for r in range(NSA_GROUP):
            sl = slice(r * tq, (r + 1) * tq)
            acc_ref[sl, :] = acc_ref[sl, :] * _lanes(alphas[r], 2 * LANES) + pv[sl]
        return c

    lax.fori_loop(lo, hi, step, 0)
    acc = acc_ref[...]
    return acc[:, :LANES] / jnp.maximum(acc[:, LANES:], 1e-30)


def _nsa_prompt_body(q_ref, kc_ref, vc_ref, ks_ref, vs_ref, kw_ref, vw_ref, gate_ref, o_ref, acc_ref, m_ref,
                     *, n_cmp, scale):
    tq = q_ref.shape[0]
    seq = ks_ref.shape[0]
    n_sel = seq // SEL_BLOCK
    qi = pl.program_id(2)
    q0 = qi * tq
    q = q_ref[...]
    qs = jnp.concatenate([q[:, r * HEAD_DIM:(r + 1) * HEAD_DIM] for r in range(NSA_GROUP)], axis=0)
    kc = kc_ref[0, 0, 0].astype(BF16)

    n_idx = lax.broadcasted_iota(I32, (LANES, tq), 0)
    qpos_l = q0 + lax.broadcasted_iota(I32, (LANES, tq), 1)
    vis_t = jnp.where(n_idx < n_cmp, n_idx * CMP_STRIDE + CMP_BLOCK - 1, seq) <= qpos_l
    s_t = _dot_nt(kc, qs)
    p_sum = jnp.zeros((LANES, tq), F32)
    for r in range(NSA_GROUP):
        s_r = jnp.where(vis_t, s_t[:, r * tq:(r + 1) * tq] * scale, NEG_INF)
        e = jnp.where(vis_t, jnp.exp(s_r - jnp.max(s_r, axis=0, keepdims=True)), 0.0)
        p_sum = p_sum + e / jnp.maximum(jnp.sum(e, axis=0, keepdims=True), 1e-30)
    m_row = lax.broadcasted_iota(I32, (LANES, LANES), 0)
    n_col = lax.broadcasted_iota(I32, (LANES, LANES), 1)
    ov_t = jnp.where(_overlap(n_col, m_row, n_cmp, n_sel), 1.0, 0.0).astype(BF16)
    imp = sum(_dot(ov_t, part) for part in _split3(p_sum))[:n_sel]
    m_idx = lax.broadcasted_iota(I32, (n_sel, tq), 0)
    cur = (q0 + lax.broadcasted_iota(I32, (n_sel, tq), 1)) // SEL_BLOCK
    imp = _force_blocks(imp, m_idx, cur)
    rank = jnp.zeros((n_sel, tq), F32)
    for mp in range(n_sel):
        other = imp[mp:mp + 1, :]
        tie = jnp.where(m_idx > mp, 1.0, 0.0)
        rank = rank + jnp.where(other > imp, 1.0, jnp.where(other == imp, tie, 0.0))
    sel = jnp.where(rank < SEL_TOP_N, 1.0, 0.0)
    sel_t = jnp.concatenate([sel, jnp.zeros((LANES - n_sel, tq), F32)], axis=0).T.astype(BF16)

    def cmp_mask(k0):
        n = lax.broadcasted_iota(I32, (tq, LANES), 1)
        qpos = q0 + lax.broadcasted_iota(I32, (tq, LANES), 0)
        return jnp.where(n < n_cmp, n * CMP_STRIDE + CMP_BLOCK - 1, seq) <= qpos

    tk_s = 4 * LANES

    def sel_mask(k0):
        blk = (k0 + lax.broadcasted_iota(I32, (LANES, tk_s), 1)) // SEL_BLOCK
        expand = jnp.where(blk == lax.broadcasted_iota(I32, (LANES, tk_s), 0), 1.0, 0.0).astype(BF16)
        chosen = _dot(sel_t, expand)
        kpos = k0 + lax.broadcasted_iota(I32, (tq, tk_s), 1)
        qpos = q0 + lax.broadcasted_iota(I32, (tq, tk_s), 0)
        return jnp.where(kpos <= qpos, chosen, 0.0) > 0.5

    def win_mask(k0):
        kpos = k0 + lax.broadcasted_iota(I32, (tq, LANES), 1)
        qpos = q0 + lax.broadcasted_iota(I32, (tq, LANES), 0)
        back = qpos - kpos
        return jnp.where(back >= 0, back, WINDOW + 1) <= WINDOW

    o_c = _attend(qs, kc_ref.at[0, 0, 0], vc_ref.at[0, 0, 0], 0, 1, LANES, cmp_mask, acc_ref, m_ref, scale)
    o_s = _attend(qs, ks_ref, vs_ref, 0, (q0 + tq + tk_s - 1) // tk_s, tk_s, sel_mask, acc_ref, m_ref, scale)
    first_w = jnp.maximum(q0 - WINDOW, 0) // LANES
    o_w = _attend(qs, kw_ref, vw_ref, first_w, qi + 1, LANES, win_mask, acc_ref, m_ref, scale)

    gate = jax.nn.sigmoid(gate_ref[...])
    for r in range(NSA_GROUP):
        sl = slice(r * tq, (r + 1) * tq)
        o = (gate[:, 3 * r:3 * r + 1] * o_c[sl] + gate[:, 3 * r + 1:3 * r + 2] * o_s[sl]
             + gate[:, 3 * r + 2:3 * r + 3] * o_w[sl])
        o_ref[:, r * HEAD_DIM:(r + 1) * HEAD_DIM] = o.astype(o_ref.dtype)


def _nsa_prompt(q, cmp, nkv, win, gates, *, batch, seq):
    tq = LANES
    assert tq == LANES and seq % (4 * LANES) == 0
    nq = seq // tq
    gw = NSA_GROUP * HEAD_DIM
    n_cmp = (seq - CMP_BLOCK) // CMP_STRIDE + 1
    rows = lambda c: pl.BlockSpec((seq, HEAD_DIM), lambda b, g, i, c=c: (b, c * N_KV_NSA + g))
    cmp_spec = lambda kind: pl.BlockSpec((1, 1, 1, LANES, HEAD_DIM), lambda b, g, i, kind=kind: (b, kind, g, 0, 0))
    return pl.pallas_call(
        functools.partial(_nsa_prompt_body, n_cmp=n_cmp, scale=HEAD_DIM ** -0.5),
        grid=(batch, N_KV_NSA, nq),
        in_specs=[pl.BlockSpec((tq, gw), lambda b, g, i: (b * nq + i, g)),
                  cmp_spec(0), cmp_spec(1), rows(2), rows(3), rows(0), rows(1),
                  pl.BlockSpec((tq, LANES), lambda b, g, i: (b * nq + i, g))],
        out_specs=pl.BlockSpec((tq, gw), lambda b, g, i: (b * nq + i, g)),
        out_shape=jax.ShapeDtypeStruct(q.shape, BF16),
        scratch_shapes=[pltpu.VMEM((NSA_GROUP * tq, 2 * LANES), F32), pltpu.VMEM((NSA_GROUP * tq, LANES), F32)],
        compiler_params=_params("arbitrary", "arbitrary", "arbitrary"),
        name="nsa_prompt",
    )(q, cmp, cmp, nkv, nkv, win, win, gates)


def _softmax_rows(s, vis):
    s = jnp.where(vis, s, NEG_INF)
    e = jnp.where(vis, jnp.exp(s - jnp.max(s, axis=-1, keepdims=True)), 0.0)
    return e / jnp.maximum(jnp.sum(e, axis=-1, keepdims=True), 1e-30)


def _nsa_sample_body(pt_ref, *refs, n_pages, past, scale):
    del pt_ref
    pages = refs[:n_pages]
    new_ref, wst_ref, wnew_ref, q_ref, gate_ref, pe_ref, w1_ref, w2_ref, o_ref = refs[n_pages:]
    slabs = 4 * N_KV_NSA
    page = pages[0].shape[1] // slabs
    pos = past
    n_keys = past + 1
    n_cmp = (n_keys - CMP_BLOCK) // CMP_STRIDE + 1
    n_sel = -(-n_keys // SEL_BLOCK)
    w_buf = wst_ref.shape[1] // (2 * N_KV_NSA)
    gd = N_KV_NSA * HEAD_DIM
    r8 = NSA_GROUP

    cmp = []
    for kind in range(2):
        xs = []
        for g in range(N_KV_NSA):
            slab = kind * N_KV_NSA + g
            cols = []
            for l in range(CMP_STRIDE):
                cols.append(jnp.concatenate(
                    [pg[0, pl.ds(l * slabs + slab, page // CMP_STRIDE, stride=CMP_STRIDE * slabs), :] for pg in pages],
                    axis=0))
            xs.append(jnp.concatenate(cols, axis=1))
        cmp.append(_compress_rows(jnp.concatenate(xs, axis=0), pe_ref, w1_ref, w2_ref, kind, n_cmp))

    lane = lax.broadcasted_iota(I32, (r8, LANES), 1)
    sq_row = lax.broadcasted_iota(I32, (LANES, LANES), 0)
    sq_col = lax.broadcasted_iota(I32, (LANES, LANES), 1)
    ov = jnp.where(_overlap(sq_row, sq_col, n_cmp, n_sel), 1.0, 0.0).astype(BF16)
    lower_first = jnp.where(sq_row < sq_col, 1.0, 0.0)
    n_all = past + LANES
    tok = lax.broadcasted_iota(I32, (r8, n_all), 1)
    expand = jnp.where(lax.broadcasted_iota(I32, (LANES, n_all), 1) // SEL_BLOCK
                       == lax.broadcasted_iota(I32, (LANES, n_all), 0), 1.0, 0.0).astype(BF16)
    widx = lax.broadcasted_iota(I32, (r8, w_buf + LANES), 1)
    gate = jax.nn.sigmoid(gate_ref[0])

    def with_new(past_rows, new_row):
        return jnp.concatenate([past_rows, jnp.broadcast_to(new_row, (LANES, HEAD_DIM))], axis=0).astype(BF16)

    for g in range(N_KV_NSA):
        q = q_ref[0, g * r8:(g + 1) * r8, :].astype(BF16)
        kc = cmp[0][g * LANES:(g + 1) * LANES].astype(BF16)
        vc = cmp[1][g * LANES:(g + 1) * LANES].astype(BF16)
        vis_c = jnp.where(lane < n_cmp, lane * CMP_STRIDE + CMP_BLOCK - 1, pos + 1) <= pos
        p_c = _softmax_rows(_dot_nt(q, kc) * scale, vis_c)
        o_c = _dot(p_c.astype(BF16), vc)

        p_sum = jnp.broadcast_to(jnp.sum(p_c, axis=0, keepdims=True), (r8, LANES))
        imp = sum(_dot(part, ov) for part in _split3(p_sum))
        imp = _force_blocks(imp, lane, pos // SEL_BLOCK)
        imp_b = jnp.broadcast_to(imp[0:1], (LANES, LANES))
        imp_c = imp_b.T
        beats = jnp.where(imp_c > imp_b, 1.0, jnp.where(imp_c == imp_b, lower_first, 0.0))
        rank = jnp.sum(beats, axis=0, keepdims=True)
        sel = jnp.broadcast_to(jnp.where(rank < SEL_TOP_N, 1.0, 0.0), (r8, LANES)).astype(BF16)
        vis_s = jnp.where(tok <= pos, _dot(sel, expand), 0.0) > 0.5

        c_k = (2 * N_KV_NSA + g) * HEAD_DIM
        c_v = (3 * N_KV_NSA + g) * HEAD_DIM
        past_rows = lambda slab: jnp.concatenate([pg[0, pl.ds(slab, page, stride=slabs), :] for pg in pages], axis=0)
        k_s = with_new(past_rows(2 * N_KV_NSA + g), new_ref[0, :, c_k:c_k + HEAD_DIM])
        v_s = with_new(past_rows(3 * N_KV_NSA + g), new_ref[0, :, c_v:c_v + HEAD_DIM])
        p_s = _softmax_rows(_dot_nt(q, k_s) * scale, vis_s)
        o_s = _dot(p_s.astype(BF16), v_s)

        win_rows = lambda kv: wst_ref[0, pl.ds(kv * N_KV_NSA + g, w_buf, stride=2 * N_KV_NSA), :]
        k_w = with_new(win_rows(0), wnew_ref[0, :, g * HEAD_DIM:(g + 1) * HEAD_DIM])
        v_w = with_new(win_rows(1), wnew_ref[0, :, gd + g * HEAD_DIM:gd + (g + 1) * HEAD_DIM])
        p_w = _softmax_rows(_dot_nt(q, k_w) * scale, widx <= w_buf)
        o_w = _dot(p_w.astype(BF16), v_w)

        gt = gate[g * r8:(g + 1) * r8]
        o_ref[0, g * r8:(g + 1) * r8, :] = gt[:, 0:1] * o_c + gt[:, 1:2] * o_s + gt[:, 2:3] * o_w


def _nsa_sample(q, gates, new_nkv, win_state, new_win, cache, page_table, pe, w1, w2):
    s, n_pages = page_table.shape
    page_rows = cache.shape[1]
    past = n_pages * page_rows // (4 * N_KV_NSA)
    assert past // CMP_STRIDE == LANES and win_state.shape[1] <= WINDOW * 2 * N_KV_NSA
    page_spec = lambda p: pl.BlockSpec((1, page_rows, HEAD_DIM), lambda b, pt, p=p: (pt[b, p], 0, 0))
    per_seq = lambda a: pl.BlockSpec((1,) + a.shape[1:], lambda b, pt: (b, 0, 0))
    whole = lambda a: pl.BlockSpec(a.shape, lambda b, pt, n=a.ndim: (0,) * n)
    grid_spec = pltpu.PrefetchScalarGridSpec(
        num_scalar_prefetch=1,
        grid=(s,),
        in_specs=[page_spec(p) for p in range(n_pages)]
        + [per_seq(new_nkv), per_seq(win_state), per_seq(new_win), per_seq(q), per_seq(gates),
           whole(pe), whole(w1), whole(w2)],
        out_specs=pl.BlockSpec((1, N_HEADS_NSA, HEAD_DIM), lambda b, pt: (b, 0, 0)),
    )
    return pl.pallas_call(
        functools.partial(_nsa_sample_body, n_pages=n_pages, past=past, scale=HEAD_DIM ** -0.5),
        grid_spec=grid_spec,
        out_shape=jax.ShapeDtypeStruct((s, N_HEADS_NSA, HEAD_DIM), F32),
        compiler_params=_params("arbitrary"),
        name="nsa_sample",
    )(page_table, *([cache] * n_pages), new_nkv, win_state, new_win, q, gates, pe, w1, w2)


def _rope_tables(pos):
    half = HEAD_DIM // 2
    inv_freq = ROPE_THETA ** (-jnp.arange(half, dtype=F32) / half)
    ang = pos.astype(F32)[:, None] * inv_freq[None, :]
    cos, sin = jnp.cos(ang), jnp.sin(ang)
    return jnp.concatenate([cos, cos], axis=1), jnp.concatenate([-sin, sin], axis=1)


def _prepare_weights(w_in, cmp_k_pe, cmp_k_w1, cmp_k_w2, cmp_v_pe, cmp_v_w1, cmp_v_w2,
                     w_branch_sb, w_branch_nsa, w_out, w_gate, w_up, w_down):
    d_sb = N_HEADS_SB * HEAD_DIM
    d_nsa = N_HEADS_NSA * HEAD_DIM
    d_kv = N_KV_NSA * HEAD_DIM
    cuts = [0, d_sb, 3 * d_sb, 3 * d_sb + d_nsa, 3 * d_sb + d_nsa + 4 * d_kv, 3 * d_sb + d_nsa + 6 * d_kv]
    w_in = w_in.astype(BF16)
    n_gate = 3 * N_HEADS_NSA
    w_ng = w_in[:, cuts[5]:cuts[5] + n_gate].reshape(-1, N_KV_NSA, 3 * NSA_GROUP)
    w_ng = jnp.pad(w_ng, ((0, 0), (0, 0), (0, LANES - 3 * NSA_GROUP))).reshape(-1, N_KV_NSA * LANES)
    d_ff = w_gate.shape[1]
    ff_pad = -d_ff % FF_PAD_MULTIPLE
    half = CMP_STRIDE * HEAD_DIM
    halves = lambda w1: w1.reshape(2, half, -1)
    return dict(
        w_in=w_in, cuts=dict(sb_q=(cuts[0], cuts[1]), sb_kv=(cuts[1], cuts[2]), nsa_q=(cuts[2], cuts[3]),
                             nsa_kv=(cuts[3], cuts[4]), nsa_win=(cuts[4], cuts[5])),
        nsa_gate=w_ng, merge=w_in[:, cuts[5] + n_gate:],
        pe=jnp.stack([cmp_k_pe.reshape(2, 1, half), cmp_v_pe.reshape(2, 1, half)]),
        w1=jnp.stack([halves(cmp_k_w1), halves(cmp_v_w1)]).astype(BF16),
        w2=jnp.stack([cmp_k_w2, cmp_v_w2]).astype(BF16),
        branch_sb=w_branch_sb.astype(BF16), branch_nsa=w_branch_nsa.astype(BF16), out=w_out.astype(BF16),
        gate=jnp.pad(w_gate, ((0, 0), (0, ff_pad))).astype(BF16), up=jnp.pad(w_up, ((0, 0), (0, ff_pad))).astype(BF16),
        down=jnp.pad(w_down, ((0, ff_pad), (0, 0))).astype(BF16),
    )


def _project(x, g_attn, w, cos, sin, tm, q_dtype):
    h = _rmsnorm(x, g_attn, BF16, min(tm, 256))
    mm = functools.partial(_matmul, h, tm=tm)
    seg = lambda s, dt: mm(w["w_in"], dt, tn=512, cols=w["cuts"][s], name="proj_" + s)
    rope = lambda s, dt, slabs: _matmul_rope(h, w["w_in"], cos, sin, dt, tm=tm, tn=4 * LANES, rope_slabs=slabs,
                                             cols=w["cuts"][s], name="proj_" + s)
    k_then_v = (True, True, False, False)
    return dict(
        sb_q=seg("sb_q", q_dtype), sb_kv=seg("sb_kv", F32),
        nsa_q=rope("nsa_q", q_dtype, (True,) * 4), nsa_kv=rope("nsa_kv", F32, k_then_v),
        nsa_win=rope("nsa_win", F32, k_then_v),
        nsa_gate=mm(w["nsa_gate"], F32, tn=256, name="proj_nsa_gate"),
        merge=mm(w["merge"], F32, tn=512, name="proj_merge"),
    )


def _finish(x, o_sb, o_nsa, merge, w, g_ffn, g_final, tm):
    merged = _matmul_merge(o_sb, w["branch_sb"], o_nsa, w["branch_nsa"], merge, tm=tm, tn=512, name="branch_merge")
    x = _matmul_residual(merged, w["out"], x, tm=tm, tn=512, tk=merged.shape[1], name="out_proj")
    h = _rmsnorm(x, g_ffn, BF16, min(tm, 256))
    ff = _matmul_gate_up(h, w["gate"], w["up"], tm=tm, tn=512, name="ffn_gate_up")
    x = _matmul_residual(ff, w["down"], x, tm=tm, tn=1024, tk=ff.shape[1] // 8, name="ffn_down")
    return _rmsnorm(x, g_final, F32, min(tm, 256))


def kernel(x_prompt, x_sample, cache_sb_kv, cache_nsa_kv, state_nsa_win, page_table, g_attn, w_in, cmp_k_pe, cmp_k_w1, cmp_k_w2, cmp_v_pe, cmp_v_w1, cmp_v_w2, w_branch_sb, w_branch_nsa, w_out, g_ffn, w_gate, w_up, w_down, g_final):
    depth = w_in.shape[0]
    assert depth == 1
    batch, seq, d_model = x_prompt.shape
    n_dec = x_sample.shape[0]
    assert x_sample.shape[1] == 1
    n_pages = page_table.shape[1]
    page = cache_sb_kv.shape[2]
    past = n_pages * page
    win_len = min(WINDOW, seq)
    layer = 0
    w = _prepare_weights(w_in[layer], cmp_k_pe[layer], cmp_k_w1[layer], cmp_k_w2[layer], cmp_v_pe[layer],
                         cmp_v_w1[layer], cmp_v_w2[layer], w_branch_sb[layer], w_branch_nsa[layer], w_out[layer],
                         w_gate[layer], w_up[layer], w_down[layer])

    xp = x_prompt.reshape(batch * seq, d_model)
    cos_p, sin_p = _rope_tables(jnp.arange(seq, dtype=I32))
    pp = _project(xp, g_attn[layer], w, cos_p, sin_p, 1024, BF16)
    o_sb = _sb_prompt(pp["sb_q"], pp["sb_kv"], batch=batch, seq=seq)
    cmp_p = _compress_prompt(pp["nsa_kv"], w["pe"], w["w1"], w["w2"], batch=batch, seq=seq)
    o_nsa = _nsa_prompt(pp["nsa_q"], cmp_p, pp["nsa_kv"], pp["nsa_win"], pp["nsa_gate"], batch=batch, seq=seq)
    y_prompt = _finish(xp, o_sb, o_nsa, pp["merge"], w, g_ffn[layer], g_final, 1024)

    xs = x_sample.reshape(n_dec, d_model)
    cos_s, sin_s = _rope_tables(jnp.full((n_dec,), past, dtype=I32))
    ps = _project(xs, g_attn[layer], w, cos_s, sin_s, n_dec, F32)
    o_sb_s = _sb_sample(ps["sb_q"].reshape(n_dec, 1, -1), cache_sb_kv, page_table)
    nsa_cache = cache_nsa_kv[layer].reshape(cache_nsa_kv.shape[1], -1, HEAD_DIM)
    gate_s = ps["nsa_gate"].reshape(n_dec, N_KV_NSA, LANES)[:, :, :3 * NSA_GROUP].reshape(n_dec, N_HEADS_NSA, 3)
    gate_s = jnp.pad(gate_s, ((0, 0), (0, 0), (0, LANES - 3)))
    win_state = state_nsa_win[layer].reshape(n_dec, -1, HEAD_DIM)
    o_nsa_s = _nsa_sample(ps["nsa_q"].reshape(n_dec, N_HEADS_NSA, HEAD_DIM), gate_s,
                          ps["nsa_kv"].reshape(n_dec, 1, -1), win_state, ps["nsa_win"].reshape(n_dec, 1, -1),
                          nsa_cache, page_table, w["pe"], w["w1"], w["w2"])
    y_sample = _finish(xs, o_sb_s.reshape(n_dec, -1).astype(BF16), o_nsa_s.reshape(n_dec, -1).astype(BF16),
                       ps["merge"], w, g_ffn[layer], g_final, n_dec)

    new_win_s = jnp.concatenate([state_nsa_win[layer, :, 1:],
                                 ps["nsa_win"].reshape(n_dec, 1, 2, N_KV_NSA, HEAD_DIM)], axis=1)
    return (
        y_prompt.reshape(batch, seq, d_model),
        y_sample.reshape(n_dec, 1, d_model),
        pp["sb_kv"].reshape(depth, batch, seq, 2, N_HEADS_SB, HEAD_DIM),
        ps["sb_kv"].reshape(depth, n_dec, 1, 2, N_HEADS_SB, HEAD_DIM),
        pp["nsa_kv"].reshape(depth, batch, seq, 4, N_KV_NSA, HEAD_DIM),
        ps["nsa_kv"].reshape(depth, n_dec, 1, 4, N_KV_NSA, HEAD_DIM),
        pp["nsa_win"].reshape(batch, seq, -1)[:, seq - win_len:].reshape(depth, batch, win_len, 2, N_KV_NSA, HEAD_DIM),
        new_win_s[None],
    )
```

```python
import functools

import jax
import jax.numpy as jnp
from jax import lax
from jax.experimental import pallas as pl
from jax.experimental.pallas import tpu as pltpu

F32 = jnp.float32
BF16 = jnp.bfloat16
I32 = jnp.int32

HEAD_DIM = 128
N_HEADS_SB = 16
N_HEADS_NSA = 16
N_KV_NSA = 2
NSA_GROUP = N_HEADS_NSA // N_KV_NSA
CMP_BLOCK = 32
CMP_STRIDE = 16
SEL_BLOCK = 64
SEL_TOP_N = 16
WINDOW = 512
ROPE_THETA = 10000.0
RMS_EPS = 1e-6
NEG_INF = -1e30
BIG = 1e30
LANES = 128
VMEM_LIMIT_BYTES = 48 * 2**20


def _params(*sem):
    return pltpu.CompilerParams(dimension_semantics=sem, vmem_limit_bytes=VMEM_LIMIT_BYTES)


def _dot(a, b):
    return jnp.dot(a, b, preferred_element_type=F32)


def _dot_nt(a, b):
    return lax.dot_general(a, b, (((1,), (1,)), ((), ())), preferred_element_type=F32)


def _lanes(x, n):
    return x if n == LANES else jnp.concatenate([x] * (n // LANES), axis=1)


def _split2(x):
    hi = x.astype(BF16)
    lo = (x - hi.astype(F32)).astype(BF16)
    return hi, lo


def _split3(x):
    hi = x.astype(BF16)
    r1 = x - hi.astype(F32)
    mid = r1.astype(BF16)
    lo = (r1 - mid.astype(F32)).astype(BF16)
    return hi, mid, lo


def _rmsnorm_body(x_ref, g_ref, o_ref):
    x = x_ref[...]
    inv = lax.rsqrt(jnp.mean(x * x, axis=-1, keepdims=True) + RMS_EPS)
    o_ref[...] = (x * inv * g_ref[...]).astype(o_ref.dtype)


def _rmsnorm(x, g, out_dtype, tm):
    m, d = x.shape
    return pl.pallas_call(
        _rmsnorm_body,
        grid=(m // tm,),
        in_specs=[pl.BlockSpec((tm, d), lambda i: (i, 0)), pl.BlockSpec((1, d), lambda i: (0, 0))],
        out_specs=pl.BlockSpec((tm, d), lambda i: (i, 0)),
        out_shape=jax.ShapeDtypeStruct((m, d), out_dtype),
        compiler_params=_params("arbitrary"),
        name="rmsnorm",
    )(x, g.reshape(1, d))


def _rope_tile(y, cos, sin, rope_slabs):
    out = []
    for c, roped in enumerate(rope_slabs):
        x = y[:, c * LANES:(c + 1) * LANES]
        out.append(x * cos + pltpu.roll(x, HEAD_DIM // 2, 1) * sin if roped else x)
    return out[0] if len(out) == 1 else jnp.concatenate(out, axis=1)


def _mm_body(a_ref, w_ref, o_ref):
    o_ref[...] = _dot(a_ref[...], w_ref[...]).astype(o_ref.dtype)


def _mm_rope_body(a_ref, w_ref, cos_ref, sin_ref, o_ref, *, rope_slabs):
    y = _dot(a_ref[...], w_ref[...])
    o_ref[...] = _rope_tile(y, cos_ref[...], sin_ref[...], rope_slabs).astype(o_ref.dtype)


def _mm_res_body(a_ref, w_ref, r_ref, o_ref):
    @pl.when(pl.program_id(2) == 0)
    def _():
        o_ref[...] = r_ref[...]

    o_ref[...] += _dot(a_ref[...], w_ref[...])


def _mm_gate_up_body(a_ref, wg_ref, wu_ref, o_ref):
    a = a_ref[...]
    o_ref[...] = (jax.nn.silu(_dot(a, wg_ref[...])) * _dot(a, wu_ref[...])).astype(o_ref.dtype)


def _mm_merge_body(a1_ref, w1_ref, a2_ref, w2_ref, g1_ref, g2_ref, o_ref):
    y1 = _dot(a1_ref[...], w1_ref[...])
    y2 = _dot(a2_ref[...], w2_ref[...])
    o_ref[...] = (jax.nn.sigmoid(g1_ref[...]) * y1 + jax.nn.sigmoid(g2_ref[...]) * y2).astype(o_ref.dtype)


def _tile(n, pref):
    return pref if n % pref == 0 else n


def _matmul(a, w, out_dtype, *, tm, tn, name, cols=None):
    m, k = a.shape
    c0, c1 = cols or (0, w.shape[1])
    n = c1 - c0
    tm, tn = _tile(m, tm), _tile(n, tn)
    j0 = c0 // tn
    assert j0 * tn == c0
    return pl.pallas_call(
        _mm_body,
        grid=(m // tm, n // tn),
        in_specs=[pl.BlockSpec((tm, k), lambda i, j: (i, 0)), pl.BlockSpec((k, tn), lambda i, j: (0, j0 + j))],
        out_specs=pl.BlockSpec((tm, tn), lambda i, j: (i, j)),
        out_shape=jax.ShapeDtypeStruct((m, n), out_dtype),
        compiler_params=_params("arbitrary", "arbitrary"),
        name=name,
    )(a, w)


def _matmul_rope(a, w, cos, sin, out_dtype, *, tm, tn, rope_slabs, name, cols):
    m, k = a.shape
    c0, c1 = cols
    n = c1 - c0
    tm, tn = _tile(m, tm), _tile(n, tn)
    j0 = c0 // tn
    assert j0 * tn == c0
    pos_tiles = cos.shape[0] // tm
    return pl.pallas_call(
        functools.partial(_mm_rope_body, rope_slabs=rope_slabs),
        grid=(m // tm, n // tn),
        in_specs=[pl.BlockSpec((tm, k), lambda i, j: (i, 0)), pl.BlockSpec((k, tn), lambda i, j: (0, j0 + j)),
                  pl.BlockSpec((tm, LANES), lambda i, j: (i % pos_tiles, 0)),
                  pl.BlockSpec((tm, LANES), lambda i, j: (i % pos_tiles, 0))],
        out_specs=pl.BlockSpec((tm, tn), lambda i, j: (i, j)),
        out_shape=jax.ShapeDtypeStruct((m, n), out_dtype),
        compiler_params=_params("arbitrary", "arbitrary"),
        name=name,
    )(a, w, cos, sin)


def _matmul_residual(a, w, res, *, tm, tn, tk, name):
    m, k = a.shape
    n = w.shape[1]
    tm, tn, tk = _tile(m, tm), _tile(n, tn), _tile(k, tk)
    return pl.pallas_call(
        _mm_res_body,
        grid=(m // tm, n // tn, k // tk),
        in_specs=[pl.BlockSpec((tm, tk), lambda i, j, kk: (i, kk)), pl.BlockSpec((tk, tn), lambda i, j, kk: (kk, j)),
                  pl.BlockSpec((tm, tn), lambda i, j, kk: (i, j))],
        out_specs=pl.BlockSpec((tm, tn), lambda i, j, kk: (i, j)),
        out_shape=jax.ShapeDtypeStruct((m, n), F32),
        compiler_params=_params("arbitrary", "arbitrary", "arbitrary"),
        name=name,
    )(a, w, res)


def _matmul_gate_up(a, wg, wu, *, tm, tn, name):
    m, k = a.shape
    n = wg.shape[1]
    tm = _tile(m, tm)
    return pl.pallas_call(
        _mm_gate_up_body,
        grid=(m // tm, pl.cdiv(n, tn)),
        in_specs=[pl.BlockSpec((tm, k), lambda i, j: (i, 0)), pl.BlockSpec((k, tn), lambda i, j: (0, j)),
                  pl.BlockSpec((k, tn), lambda i, j: (0, j))],
        out_specs=pl.BlockSpec((tm, tn), lambda i, j: (i, j)),
        out_shape=jax.ShapeDtypeStruct((m, n), BF16),
        compiler_params=_params("arbitrary", "arbitrary"),
        name=name,
    )(a, wg, wu)


def _matmul_merge(a1, w1, a2, w2, gates, *, tm, tn, name):
    m, k = a1.shape
    n = w1.shape[1]
    tm, tn = _tile(m, tm), _tile(n, tn)
    nj = n // tn
    return pl.pallas_call(
        _mm_merge_body,
        grid=(m // tm, nj),
        in_specs=[pl.BlockSpec((tm, k), lambda i, j: (i, 0)), pl.BlockSpec((k, tn), lambda i, j: (0, j)),
                  pl.BlockSpec((tm, k), lambda i, j: (i, 0)), pl.BlockSpec((k, tn), lambda i, j: (0, j)),
                  pl.BlockSpec((tm, tn), lambda i, j: (i, j)), pl.BlockSpec((tm, tn), lambda i, j: (i, j + nj))],
        out_specs=pl.BlockSpec((tm, tn), lambda i, j: (i, j)),
        out_shape=jax.ShapeDtypeStruct((m, n), BF16),
        compiler_params=_params("arbitrary", "arbitrary"),
        name=name,
    )(a1, w1, a2, w2, gates, gates)


def _suffix_matrix(t):
    row = lax.broadcasted_iota(I32, (2 * t, 2 * t), 0) % t
    col = lax.broadcasted_iota(I32, (2 * t, 2 * t), 1)
    return jnp.where((row > col) | (col >= t), 1.0, 0.0).astype(BF16)


def _log_one_minus_beta(z):
    return jnp.minimum(-z, 0.0) - jnp.log(1.0 + jnp.exp(-jnp.abs(z)))


def _sb_block(z, v, u, carry, causal):
    sub = LANES
    n_sub = z.shape[1] // sub
    l1m = _log_one_minus_beta(z)
    if causal is not None:
        l1m = jnp.where(causal, l1m, 0.0)
    hi, lo = _split2(l1m)
    tails = [None] * n_sub
    for j in reversed(range(n_sub)):
        sl = slice(j * sub, (j + 1) * sub)
        sums = _dot(jnp.concatenate([hi[:, sl], lo[:, sl]], axis=1), u)
        tails[j] = sums[:, :sub] + carry
        carry = carry + sums[:, sub:]
    tail = tails[0] if n_sub == 1 else jnp.concatenate(tails, axis=1)
    a = jnp.exp(l1m + z + tail)
    if causal is not None:
        a = jnp.where(causal, a, 0.0)
    return _dot(a.astype(BF16), v), carry


def _sb_prompt_body(q_ref, k_ref, v_ref, o_ref, *, tq, tk, chains, scale):
    seq = q_ref.shape[0]
    rows = tq // chains
    u = _suffix_matrix(LANES)
    col_minus_row = lax.broadcasted_iota(I32, (rows, tk), 1) - lax.broadcasted_iota(I32, (rows, tk), 0)

    def load_kv(kb):
        k0 = pl.multiple_of(kb * tk, tk)
        return k_ref[pl.ds(k0, tk), :].astype(BF16), v_ref[pl.ds(k0, tk), :].astype(BF16)

    def q_tile(qi, _):
        q0 = pl.multiple_of(qi * tq, tq)
        qs = [q_ref[pl.ds(pl.multiple_of(q0 + c * rows, rows), rows), :] for c in range(chains)]
        kb_diag = q0 // tk
        k, v = load_kv(kb_diag)
        state = []
        for c in range(chains):
            causal = col_minus_row < q0 + c * rows - kb_diag * tk
            state += _sb_block(_dot_nt(qs[c], k) * scale, v, u, jnp.zeros((rows, LANES), F32), causal)

        def left(t, st):
            k, v = load_kv(kb_diag - 1 - t)
            new = []
            for c in range(chains):
                pv, carry = _sb_block(_dot_nt(qs[c], k) * scale, v, u, st[2 * c + 1], None)
                new += [st[2 * c] + pv, carry]
            return tuple(new)

        state = lax.fori_loop(0, kb_diag, left, tuple(state))
        for c in range(chains):
            o_ref[pl.ds(pl.multiple_of(q0 + c * rows, rows), rows), :] = state[2 * c].astype(o_ref.dtype)
        return 0

    lax.fori_loop(0, seq // tq, q_tile, 0)


def _sb_prompt(q, kv, *, batch, seq):
    h = N_HEADS_SB
    tq, tk, chains = 4 * LANES, 4 * LANES, 2
    assert tk % tq == 0 and seq % tk == 0
    return pl.pallas_call(
        functools.partial(_sb_prompt_body, tq=tq, tk=tk, chains=chains, scale=HEAD_DIM ** -0.5),
        grid=(batch, h),
        in_specs=[pl.BlockSpec((seq, HEAD_DIM), lambda b, i: (b, i)),
                  pl.BlockSpec((seq, HEAD_DIM), lambda b, i: (b, i)),
                  pl.BlockSpec((seq, HEAD_DIM), lambda b, i: (b, h + i))],
        out_specs=pl.BlockSpec((seq, HEAD_DIM), lambda b, i: (b, i)),
        out_shape=jax.ShapeDtypeStruct(q.shape, BF16),
        compiler_params=_params("arbitrary", "arbitrary"),
        name="sb_prompt",
    )(q, kv, kv)


SB_PAGES_PER_STEP = 8
SUBLANES = 8


def _sb_sample_body(pt_ref, q_ref, cache_ref, o_ref, buf_ref, sem_ref, acc_ref, carry_ref, *, n_pages, scale):
    b, pp = pl.program_id(0), pl.program_id(1)
    steps = pl.num_programs(1)
    t = b * steps + pp
    slot = t % 2
    d = N_HEADS_SB * HEAD_DIM
    page = cache_ref.shape[2]
    halves = N_HEADS_SB // SUBLANES
    per_page = 2 * halves

    def copies(seq, step, to_slot):
        out = []
        for j in range(SB_PAGES_PER_STEP):
            pg = pt_ref[seq, n_pages - 1 - (step * SB_PAGES_PER_STEP + j)]
            for kv in range(2):
                for half in range(halves):
                    out.append(pltpu.make_async_copy(
                        cache_ref.at[0, pg, :, kv, pl.ds(half * SUBLANES, SUBLANES), :],
                        buf_ref.at[to_slot, j * per_page + kv * halves + half], sem_ref.at[to_slot]))
        return out

    @pl.when(t == 0)
    def _():
        for c in copies(0, 0, 0):
            c.start()

    @pl.when(t + 1 < pl.num_programs(0) * steps)
    def _():
        for c in copies((t + 1) // steps, (t + 1) % steps, 1 - slot):
            c.start()

    for c in copies(b, pp, slot):
        c.wait()

    @pl.when(pp == 0)
    def _():
        acc_ref[...] = jnp.zeros_like(acc_ref)
        carry_ref[...] = jnp.zeros_like(carry_ref)

    head = lax.broadcasted_iota(I32, (N_HEADS_SB, d), 0)
    lane_head = lax.broadcasted_iota(I32, (N_HEADS_SB, d), 1) // HEAD_DIM
    own = head == lane_head
    q_bd = jnp.where(own, jnp.broadcast_to(q_ref[0], (N_HEADS_SB, d)), 0.0).astype(BF16)
    u = _suffix_matrix(LANES)

    block_rows = page * SUBLANES
    flat = buf_ref.reshape(2 * SB_PAGES_PER_STEP * per_page * block_rows, HEAD_DIM)

    def heads_to_lanes(first_block):
        base = (slot * SB_PAGES_PER_STEP * per_page + first_block) * block_rows
        return jnp.concatenate([flat[pl.ds(base + i * block_rows + h, page, stride=SUBLANES), :]
                                for i in range(halves) for h in range(SUBLANES)], axis=1).astype(BF16)

    acc, carry = acc_ref[...], carry_ref[...]
    for j in range(SB_PAGES_PER_STEP):
        k = heads_to_lanes(j * per_page)
        v = heads_to_lanes(j * per_page + halves)
        pv, carry = _sb_block(_dot_nt(q_bd, k) * scale, v, u, carry, None)
        acc = acc + pv
    acc_ref[...] = acc
    carry_ref[...] = carry

    @pl.when(pp == pl.num_programs(1) - 1)
    def _():
        o_ref[0] = jnp.sum(jnp.where(own, acc, 0.0), axis=0, keepdims=True)


def _sb_sample(q, cache, page_table):
    s, n_pages = page_table.shape
    d = N_HEADS_SB * HEAD_DIM
    page = cache.shape[2]
    assert page == LANES and n_pages % SB_PAGES_PER_STEP == 0
    steps = n_pages // SB_PAGES_PER_STEP
    n_blocks = SB_PAGES_PER_STEP * 2 * N_HEADS_SB // SUBLANES
    grid_spec = pltpu.PrefetchScalarGridSpec(
        num_scalar_prefetch=1,
        grid=(s, steps),
        in_specs=[pl.BlockSpec((1, 1, d), lambda b, p, pt: (b, 0, 0)), pl.BlockSpec(memory_space=pl.ANY)],
        out_specs=pl.BlockSpec((1, 1, d), lambda b, p, pt: (b, 0, 0)),
        scratch_shapes=[pltpu.VMEM((2, n_blocks, page, SUBLANES, HEAD_DIM), F32), pltpu.SemaphoreType.DMA((2,)),
                        pltpu.VMEM((N_HEADS_SB, d), F32), pltpu.VMEM((N_HEADS_SB, LANES), F32)],
    )
    return pl.pallas_call(
        functools.partial(_sb_sample_body, n_pages=n_pages, scale=HEAD_DIM ** -0.5),
        grid_spec=grid_spec,
        out_shape=jax.ShapeDtypeStruct((s, 1, d), F32),
        compiler_params=_params("arbitrary", "arbitrary"),
        name="sb_sample",
    )(page_table, q, cache)


def _compress_rows(x, pe_ref, w1_ref, w2_ref, kind, n_cmp):
    c = LANES
    pa = _dot((x + pe_ref[kind, 0]).astype(BF16), w1_ref[kind, 0])
    pb = _dot((x + pe_ref[kind, 1]).astype(BF16), w1_ref[kind, 1])
    pb = jnp.concatenate([pltpu.roll(pb[i:i + c], c - 1, 0) for i in range(0, x.shape[0], c)], axis=0)
    out = _dot(jax.nn.gelu(pa + pb).astype(BF16), w2_ref[kind])
    row = lax.broadcasted_iota(I32, out.shape, 0) % c
    return jnp.where(row < n_cmp, out, 0.0)


def _compress_prompt_body(*refs, n_cmp):
    rows, (pe_ref, w1_ref, w2_ref, o_ref) = refs[:2 * N_KV_NSA], refs[2 * N_KV_NSA:]
    chunks = rows[0].shape[0] // CMP_STRIDE
    for kind in range(2):
        xs = [jnp.concatenate([rows[kind * N_KV_NSA + g][pl.ds(l, chunks, stride=CMP_STRIDE), :]
                               for l in range(CMP_STRIDE)], axis=1) for g in range(N_KV_NSA)]
        out = _compress_rows(jnp.concatenate(xs, axis=0), pe_ref, w1_ref, w2_ref, kind, n_cmp)
        for g in range(N_KV_NSA):
            o_ref[0, kind, g] = out[g * LANES:(g + 1) * LANES]


def _compress_prompt(nkv, pe, w1, w2, *, batch, seq):
    chunks = seq // CMP_STRIDE
    assert chunks == LANES
    n_cmp = (seq - CMP_BLOCK) // CMP_STRIDE + 1
    col = lambda c: pl.BlockSpec((seq, HEAD_DIM), lambda b, c=c: (b, c))
    return pl.pallas_call(
        functools.partial(_compress_prompt_body, n_cmp=n_cmp),
        grid=(batch,),
        in_specs=[col(c) for c in range(2 * N_KV_NSA)]
        + [pl.BlockSpec(pe.shape, lambda b: (0, 0, 0, 0)),
           pl.BlockSpec(w1.shape, lambda b: (0, 0, 0, 0)),
           pl.BlockSpec(w2.shape, lambda b: (0, 0, 0))],
        out_specs=pl.BlockSpec((1, 2, N_KV_NSA, chunks, HEAD_DIM), lambda b: (b, 0, 0, 0, 0)),
        out_shape=jax.ShapeDtypeStruct((batch, 2, N_KV_NSA, chunks, HEAD_DIM), F32),
        compiler_params=_params("arbitrary"),
        name="compress_prompt",
    )(*([nkv] * (2 * N_KV_NSA)), pe, w1, w2)


def _overlap(n_idx, m_idx, n_cmp, n_sel):
    d = n_idx * CMP_STRIDE - m_idx * SEL_BLOCK
    d = jnp.where(n_idx < n_cmp, d, SEL_BLOCK)
    d = jnp.where(m_idx < n_sel, d, SEL_BLOCK)
    return jnp.where(d > -CMP_BLOCK, d, SEL_BLOCK) < SEL_BLOCK


def _force_blocks(imp, m_idx, cur):
    forced = (m_idx == 0) | (m_idx == cur) | (m_idx == cur - 1)
    imp = jnp.where(forced, BIG, imp)
    return jnp.where(m_idx <= cur, imp, -BIG)


LOG2_E = 1.4426950408889634


def _attend(qs, k_ref, v_ref, lo, hi, tk, mask_fn, acc_ref, m_ref, scale, first_tile_hits_every_row, chains):
    tq = m_ref.shape[0] // NSA_GROUP
    per_chain = NSA_GROUP // chains
    c2 = scale * LOG2_E
    acc_ref[...] = jnp.zeros_like(acc_ref)
    m_ref[...] = jnp.full(m_ref.shape, NEG_INF, F32)
    ones = jnp.ones((tk, LANES), BF16)

    def step(kt, c):
        k0 = pl.multiple_of(kt * tk, tk)
        kb = k_ref[pl.ds(k0, tk), :].astype(BF16)
        vb = jnp.concatenate([v_ref[pl.ds(k0, tk), :].astype(BF16), ones], axis=1)
        mask = mask_fn(k0)
        for chain in range(chains):
            base = chain * per_chain * tq
            s = _dot_nt(qs[base:base + per_chain * tq], kb)
            ps, alphas = [], []
            for r in range(per_chain):
                sl = slice(base + r * tq, base + (r + 1) * tq)
                s_r = jnp.where(mask, s[r * tq:(r + 1) * tq], NEG_INF)
                m_prev = m_ref[sl, :]
                m_new = jnp.maximum(m_prev, jnp.max(s_r, axis=-1, keepdims=True))
                alphas.append(jnp.exp2((m_prev - m_new) * c2))
                m_ref[sl, :] = m_new
                p = jnp.exp2((s_r - _lanes(m_new, tk)) * c2)
                if not first_tile_hits_every_row:
                    p = jnp.where(mask, p, 0.0)
                ps.append(p.astype(BF16))
            pv = _dot(jnp.concatenate(ps, axis=0), vb)
            for r in range(per_chain):
                sl = slice(base + r * tq, base + (r + 1) * tq)
                acc_ref[sl, :] = acc_ref[sl, :] * _lanes(alphas[r], 2 * LANES) + pv[r * tq:(r + 1) * tq]
        return c

    lax.fori_loop(lo, hi, step, 0)
    acc = acc_ref[...]
    return acc[:, :LANES] / jnp.maximum(acc[:, LANES:], 1e-30)


def _nsa_prompt_body(q_ref, kc_ref, vc_ref, ks_ref, vs_ref, kw_ref, vw_ref, gate_ref, o_ref, acc_ref, m_ref,
                     *, n_cmp, scale):
    tq = q_ref.shape[0]
    seq = ks_ref.shape[0]
    n_sel = seq // SEL_BLOCK
    qi = pl.program_id(2)
    q0 = qi * tq
    q = q_ref[...]
    qs = jnp.concatenate([q[:, r * HEAD_DIM:(r + 1) * HEAD_DIM] for r in range(NSA_GROUP)], axis=0)
    kc = kc_ref[0, 0, 0].astype(BF16)

    n_idx = lax.broadcasted_iota(I32, (LANES, tq), 0)
    qpos_l = q0 + lax.broadcasted_iota(I32, (LANES, tq), 1)
    vis_t = jnp.where(n_idx < n_cmp, n_idx * CMP_STRIDE + CMP_BLOCK - 1, seq) <= qpos_l
    s_t = _dot_nt(kc, qs)
    p_sum = jnp.zeros((LANES, tq), F32)
    for r in range(NSA_GROUP):
        s_r = jnp.where(vis_t, s_t[:, r * tq:(r + 1) * tq] * scale, NEG_INF)
        e = jnp.where(vis_t, jnp.exp(s_r - jnp.max(s_r, axis=0, keepdims=True)), 0.0)
        p_sum = p_sum + e / jnp.maximum(jnp.sum(e, axis=0, keepdims=True), 1e-30)
    m_row = lax.broadcasted_iota(I32, (LANES, LANES), 0)
    n_col = lax.broadcasted_iota(I32, (LANES, LANES), 1)
    ov_t = jnp.where(_overlap(n_col, m_row, n_cmp, n_sel), 1.0, 0.0).astype(BF16)
    imp = sum(_dot(ov_t, part) for part in _split3(p_sum))[:n_sel]
    m_idx = lax.broadcasted_iota(I32, (n_sel, tq), 0)
    cur = (q0 + lax.broadcasted_iota(I32, (n_sel, tq), 1)) // SEL_BLOCK
    imp = _force_blocks(imp, m_idx, cur)
    rank = jnp.zeros((n_sel, tq), F32)
    for mp in range(n_sel):
        other = imp[mp:mp + 1, :]
        tie = jnp.where(m_idx > mp, 1.0, 0.0)
        rank = rank + jnp.where(other > imp, 1.0, jnp.where(other == imp, tie, 0.0))
    sel = jnp.where(rank < SEL_TOP_N, 1.0, 0.0)
    sel_t = jnp.concatenate([sel, jnp.zeros((LANES - n_sel, tq), F32)], axis=0).T.astype(BF16)

    def cmp_mask(k0):
        n = lax.broadcasted_iota(I32, (tq, LANES), 1)
        qpos = q0 + lax.broadcasted_iota(I32, (tq, LANES), 0)
        return jnp.where(n < n_cmp, n * CMP_STRIDE + CMP_BLOCK - 1, seq) <= qpos

    tk_s = 4 * LANES

    def sel_mask(k0):
        blk = (k0 + lax.broadcasted_iota(I32, (LANES, tk_s), 1)) // SEL_BLOCK
        expand = jnp.where(blk == lax.broadcasted_iota(I32, (LANES, tk_s), 0), 1.0, 0.0).astype(BF16)
        chosen = _dot(sel_t, expand)
        kpos = k0 + lax.broadcasted_iota(I32, (tq, tk_s), 1)
        qpos = q0 + lax.broadcasted_iota(I32, (tq, tk_s), 0)
        return jnp.where(kpos <= qpos, chosen, 0.0) > 0.5

    def win_mask(k0):
        kpos = k0 + lax.broadcasted_iota(I32, (tq, LANES), 1)
        qpos = q0 + lax.broadcasted_iota(I32, (tq, LANES), 0)
        back = qpos - kpos
        return jnp.where(back >= 0, back, WINDOW + 1) <= WINDOW

    o_c = _attend(qs, kc_ref.at[0, 0, 0], vc_ref.at[0, 0, 0], 0, 1, LANES, cmp_mask, acc_ref, m_ref, scale, False, 1)
    o_s = _attend(qs, ks_ref, vs_ref, 0, (q0 + tq + tk_s - 1) // tk_s, tk_s, sel_mask, acc_ref, m_ref, scale, True, 2)
    first_w = jnp.maximum(q0 - WINDOW, 0) // LANES
    o_w = _attend(qs, kw_ref, vw_ref, first_w, qi + 1, LANES, win_mask, acc_ref, m_ref, scale, True, 1)

    gate = jax.nn.sigmoid(gate_ref[...])
    for r in range(NSA_GROUP):
        sl = slice(r * tq, (r + 1) * tq)
        o = (gate[:, 3 * r:3 * r + 1] * o_c[sl] + gate[:, 3 * r + 1:3 * r + 2] * o_s[sl]
             + gate[:, 3 * r + 2:3 * r + 3] * o_w[sl])
        o_ref[:, r * HEAD_DIM:(r + 1) * HEAD_DIM] = o.astype(o_ref.dtype)


def _nsa_prompt(q, cmp, nkv, win, gates, *, batch, seq):
    tq = LANES
    assert tq == LANES and seq % (4 * LANES) == 0
    nq = seq // tq
    gw = NSA_GROUP * HEAD_DIM
    n_cmp = (seq - CMP_BLOCK) // CMP_STRIDE + 1
    rows = lambda c: pl.BlockSpec((seq, HEAD_DIM), lambda b, g, i, c=c: (b, c * N_KV_NSA + g))
    cmp_spec = lambda kind: pl.BlockSpec((1, 1, 1, LANES, HEAD_DIM), lambda b, g, i, kind=kind: (b, kind, g, 0, 0))
    return pl.pallas_call(
        functools.partial(_nsa_prompt_body, n_cmp=n_cmp, scale=HEAD_DIM ** -0.5),
        grid=(batch, N_KV_NSA, nq),
        in_specs=[pl.BlockSpec((tq, gw), lambda b, g, i: (b * nq + i, g)),
                  cmp_spec(0), cmp_spec(1), rows(2), rows(3), rows(0), rows(1),
                  pl.BlockSpec((tq, LANES), lambda b, g, i: (b * nq + i, g))],
        out_specs=pl.BlockSpec((tq, gw), lambda b, g, i: (b * nq + i, g)),
        out_shape=jax.ShapeDtypeStruct(q.shape, BF16),
        scratch_shapes=[pltpu.VMEM((NSA_GROUP * tq, 2 * LANES), F32), pltpu.VMEM((NSA_GROUP * tq, LANES), F32)],
        compiler_params=_params("arbitrary", "arbitrary", "arbitrary"),
        name="nsa_prompt",
    )(q, cmp, cmp, nkv, nkv, win, win, gates)


def _softmax_rows(s, vis):
    s = jnp.where(vis, s, NEG_INF)
    e = jnp.where(vis, jnp.exp(s - jnp.max(s, axis=-1, keepdims=True)), 0.0)
    return e / jnp.maximum(jnp.sum(e, axis=-1, keepdims=True), 1e-30)


def _nsa_sample_body(pt_ref, cache_ref, new_ref, wst_ref, wnew_ref, q_ref, gate_ref, pe_ref, w1_ref, w2_ref,
                     o_ref, buf_ref, sem_ref, *, n_pages, past, scale):
    b = pl.program_id(0)
    slot = b % 2
    slabs = 4 * N_KV_NSA
    page_chunks = cache_ref.shape[1]
    chunks = n_pages * page_chunks
    pos = past
    n_keys = past + 1
    n_cmp = (n_keys - CMP_BLOCK) // CMP_STRIDE + 1
    n_sel = -(-n_keys // SEL_BLOCK)
    w_buf = wst_ref.shape[1] // (2 * N_KV_NSA)
    gd = N_KV_NSA * HEAD_DIM
    r8 = NSA_GROUP

    def copies(seq, to_slot):
        out = []
        for p in range(n_pages):
            page = pt_ref[seq, p]
            for l in range(CMP_STRIDE):
                out.append(pltpu.make_async_copy(cache_ref.at[page, :, l],
                                                 buf_ref.at[to_slot, l, pl.ds(p * page_chunks, page_chunks)],
                                                 sem_ref.at[to_slot]))
        return out

    @pl.when(b == 0)
    def _():
        for c in copies(0, 0):
            c.start()

    @pl.when(b + 1 < pl.num_programs(0))
    def _():
        for c in copies(b + 1, 1 - slot):
            c.start()

    for c in copies(b, slot):
        c.wait()

    flat = buf_ref.reshape(2 * CMP_STRIDE * chunks * slabs, HEAD_DIM)

    def rows_of(l, slab):
        start = (slot * CMP_STRIDE + l) * (chunks * slabs) + slab
        return flat[pl.ds(start, chunks, stride=slabs), :]

    cmp = []
    for kind in range(2):
        xs = [jnp.concatenate([rows_of(l, kind * N_KV_NSA + g) for l in range(CMP_STRIDE)], axis=1)
              for g in range(N_KV_NSA)]
        cmp.append(_compress_rows(jnp.concatenate(xs, axis=0), pe_ref, w1_ref, w2_ref, kind, n_cmp))

    lane = lax.broadcasted_iota(I32, (r8, LANES), 1)
    sq_row = lax.broadcasted_iota(I32, (LANES, LANES), 0)
    sq_col = lax.broadcasted_iota(I32, (LANES, LANES), 1)
    ov = jnp.where(_overlap(sq_row, sq_col, n_cmp, n_sel), 1.0, 0.0).astype(BF16)
    lower_first = jnp.where(sq_row < sq_col, 1.0, 0.0)
    expand = jnp.where(sq_col // (SEL_BLOCK // CMP_STRIDE) == sq_row, 1.0, 0.0).astype(BF16)
    widx = lax.broadcasted_iota(I32, (r8, w_buf + LANES), 1)
    gate = jax.nn.sigmoid(gate_ref[0])

    def with_new(past_rows, new_row):
        return jnp.concatenate([past_rows, jnp.broadcast_to(new_row, (LANES, HEAD_DIM))], axis=0).astype(BF16)

    for g in range(N_KV_NSA):
        q = q_ref[0, g * r8:(g + 1) * r8, :].astype(BF16)
        kc = cmp[0][g * LANES:(g + 1) * LANES].astype(BF16)
        vc = cmp[1][g * LANES:(g + 1) * LANES].astype(BF16)
        vis_c = jnp.where(lane < n_cmp, lane * CMP_STRIDE + CMP_BLOCK - 1, pos + 1) <= pos
        p_c = _softmax_rows(_dot_nt(q, kc) * scale, vis_c)
        o_c = _dot(p_c.astype(BF16), vc)

        p_sum = jnp.broadcast_to(jnp.sum(p_c, axis=0, keepdims=True), (r8, LANES))
        imp = sum(_dot(part, ov) for part in _split3(p_sum))
        imp = _force_blocks(imp, lane, pos // SEL_BLOCK)
        imp_b = jnp.broadcast_to(imp[0:1], (LANES, LANES))
        imp_c = imp_b.T
        beats = jnp.where(imp_c > imp_b, 1.0, jnp.where(imp_c == imp_b, lower_first, 0.0))
        rank = jnp.sum(beats, axis=0, keepdims=True)
        sel = jnp.broadcast_to(jnp.where(rank < SEL_TOP_N, 1.0, 0.0), (r8, LANES))
        cur = pos // SEL_BLOCK
        new_tile = jnp.where(lane == 0, sel[:, cur:cur + 1], 0.0)
        vis_s = jnp.concatenate([_dot(sel.astype(BF16), expand)] * CMP_STRIDE + [new_tile], axis=1) > 0.5

        c_k = (2 * N_KV_NSA + g) * HEAD_DIM
        c_v = (3 * N_KV_NSA + g) * HEAD_DIM
        past_rows = lambda slab: jnp.concatenate([rows_of(l, slab) for l in range(CMP_STRIDE)], axis=0)
        k_s = with_new(past_rows(2 * N_KV_NSA + g), new_ref[0, :, c_k:c_k + HEAD_DIM])
        v_s = with_new(past_rows(3 * N_KV_NSA + g), new_ref[0, :, c_v:c_v + HEAD_DIM])
        p_s = _softmax_rows(_dot_nt(q, k_s) * scale, vis_s)
        o_s = _dot(p_s.astype(BF16), v_s)

        win_rows = lambda kv: wst_ref[0, pl.ds(kv * N_KV_NSA + g, w_buf, stride=2 * N_KV_NSA), :]
        k_w = with_new(win_rows(0), wnew_ref[0, :, g * HEAD_DIM:(g + 1) * HEAD_DIM])
        v_w = with_new(win_rows(1), wnew_ref[0, :, gd + g * HEAD_DIM:gd + (g + 1) * HEAD_DIM])
        p_w = _softmax_rows(_dot_nt(q, k_w) * scale, widx <= w_buf)
        o_w = _dot(p_w.astype(BF16), v_w)

        gt = gate[g * r8:(g + 1) * r8]
        o_ref[0, g * r8:(g + 1) * r8, :] = gt[:, 0:1] * o_c + gt[:, 1:2] * o_s + gt[:, 2:3] * o_w


def _nsa_sample(q, gates, new_nkv, win_state, new_win, cache, page_table, pe, w1, w2):
    s, n_pages = page_table.shape
    page_chunks, slabs = cache.shape[1], cache.shape[3]
    chunks = n_pages * page_chunks
    past = chunks * CMP_STRIDE
    assert chunks == LANES and cache.shape[2] == CMP_STRIDE and win_state.shape[1] <= WINDOW * 2 * N_KV_NSA
    per_seq = lambda a: pl.BlockSpec((1,) + a.shape[1:], lambda b, pt: (b, 0, 0))
    whole = lambda a: pl.BlockSpec(a.shape, lambda b, pt, n=a.ndim: (0,) * n)
    grid_spec = pltpu.PrefetchScalarGridSpec(
        num_scalar_prefetch=1,
        grid=(s,),
        in_specs=[pl.BlockSpec(memory_space=pl.ANY),
                  per_seq(new_nkv), per_seq(win_state), per_seq(new_win), per_seq(q), per_seq(gates),
                  whole(pe), whole(w1), whole(w2)],
        out_specs=pl.BlockSpec((1, N_HEADS_NSA, HEAD_DIM), lambda b, pt: (b, 0, 0)),
        scratch_shapes=[pltpu.VMEM((2, CMP_STRIDE, chunks, slabs, HEAD_DIM), F32), pltpu.SemaphoreType.DMA((2,))],
    )
    return pl.pallas_call(
        functools.partial(_nsa_sample_body, n_pages=n_pages, past=past, scale=HEAD_DIM ** -0.5),
        grid_spec=grid_spec,
        out_shape=jax.ShapeDtypeStruct((s, N_HEADS_NSA, HEAD_DIM), F32),
        compiler_params=_params("arbitrary"),
        name="nsa_sample",
    )(page_table, cache, new_nkv, win_state, new_win, q, gates, pe, w1, w2)


def _rope_tables(pos):
    half = HEAD_DIM // 2
    inv_freq = ROPE_THETA ** (-jnp.arange(half, dtype=F32) / half)
    ang = pos.astype(F32)[:, None] * inv_freq[None, :]
    cos, sin = jnp.cos(ang), jnp.sin(ang)
    return jnp.concatenate([cos, cos], axis=1), jnp.concatenate([-sin, sin], axis=1)


def _prepare_weights(w_in, cmp_k_pe, cmp_k_w1, cmp_k_w2, cmp_v_pe, cmp_v_w1, cmp_v_w2,
                     w_branch_sb, w_branch_nsa, w_out, w_gate, w_up, w_down):
    d_sb = N_HEADS_SB * HEAD_DIM
    d_nsa = N_HEADS_NSA * HEAD_DIM
    d_kv = N_KV_NSA * HEAD_DIM
    cuts = [0, d_sb, 3 * d_sb, 3 * d_sb + d_nsa, 3 * d_sb + d_nsa + 4 * d_kv, 3 * d_sb + d_nsa + 6 * d_kv]
    w_in = w_in.astype(BF16)
    n_gate = 3 * N_HEADS_NSA
    w_ng = w_in[:, cuts[5]:cuts[5] + n_gate].reshape(-1, N_KV_NSA, 3 * NSA_GROUP)
    w_ng = jnp.pad(w_ng, ((0, 0), (0, 0), (0, LANES - 3 * NSA_GROUP))).reshape(-1, N_KV_NSA * LANES)
    half = CMP_STRIDE * HEAD_DIM
    halves = lambda w1: w1.reshape(2, half, -1)
    return dict(
        w_in=w_in, cuts=dict(sb_q=(cuts[0], cuts[1]), sb_kv=(cuts[1], cuts[2]), nsa_q=(cuts[2], cuts[3]),
                             nsa_kv=(cuts[3], cuts[4]), nsa_win=(cuts[4], cuts[5])),
        nsa_gate=w_ng, merge=w_in[:, cuts[5] + n_gate:],
        pe=jnp.stack([cmp_k_pe.reshape(2, 1, half), cmp_v_pe.reshape(2, 1, half)]),
        w1=jnp.stack([halves(cmp_k_w1), halves(cmp_v_w1)]).astype(BF16),
        w2=jnp.stack([cmp_k_w2, cmp_v_w2]).astype(BF16),
        branch_sb=w_branch_sb.astype(BF16), branch_nsa=w_branch_nsa.astype(BF16), out=w_out.astype(BF16),
        gate=w_gate.astype(BF16), up=w_up.astype(BF16), down=w_down.astype(BF16),
    )


def _project(x, g_attn, w, cos, sin, tm, q_dtype):
    h = _rmsnorm(x, g_attn, BF16, min(tm, 256))
    mm = functools.partial(_matmul, h, tm=tm)
    seg = lambda s, dt: mm(w["w_in"], dt, tn=512, cols=w["cuts"][s], name="proj_" + s)
    rope = lambda s, dt, slabs: _matmul_rope(h, w["w_in"], cos, sin, dt, tm=tm, tn=4 * LANES, rope_slabs=slabs,
                                             cols=w["cuts"][s], name="proj_" + s)
    k_then_v = (True, True, False, False)
    return dict(
        sb_q=seg("sb_q", q_dtype), sb_kv=seg("sb_kv", F32),
        nsa_q=rope("nsa_q", q_dtype, (True,) * 4), nsa_kv=rope("nsa_kv", F32, k_then_v),
        nsa_win=rope("nsa_win", F32, k_then_v),
        nsa_gate=mm(w["nsa_gate"], F32, tn=256, name="proj_nsa_gate"),
        merge=mm(w["merge"], F32, tn=512, name="proj_merge"),
    )


def _finish(x, o_sb, o_nsa, merge, w, g_ffn, g_final, tm):
    merged = _matmul_merge(o_sb, w["branch_sb"], o_nsa, w["branch_nsa"], merge, tm=tm, tn=512, name="branch_merge")
    x = _matmul_residual(merged, w["out"], x, tm=tm, tn=512, tk=merged.shape[1], name="out_proj")
    h = _rmsnorm(x, g_ffn, BF16, min(tm, 256))
    ff = _matmul_gate_up(h, w["gate"], w["up"], tm=tm, tn=512, name="ffn_gate_up")
    x = _matmul_residual(ff, w["down"], x, tm=tm, tn=512, tk=ff.shape[1] // 2, name="ffn_down")
    return _rmsnorm(x, g_final, F32, min(tm, 256))


def kernel(x_prompt, x_sample, cache_sb_kv, cache_nsa_kv, state_nsa_win, page_table, g_attn, w_in, cmp_k_pe, cmp_k_w1, cmp_k_w2, cmp_v_pe, cmp_v_w1, cmp_v_w2, w_branch_sb, w_branch_nsa, w_out, g_ffn, w_gate, w_up, w_down, g_final):
    depth = w_in.shape[0]
    assert depth == 1
    batch, seq, d_model = x_prompt.shape
    n_dec = x_sample.shape[0]
    assert x_sample.shape[1] == 1
    n_pages = page_table.shape[1]
    page = cache_sb_kv.shape[2]
    past = n_pages * page
    win_len = min(WINDOW, seq)
    layer = 0
    w = _prepare_weights(w_in[layer], cmp_k_pe[layer], cmp_k_w1[layer], cmp_k_w2[layer], cmp_v_pe[layer],
                         cmp_v_w1[layer], cmp_v_w2[layer], w_branch_sb[layer], w_branch_nsa[layer], w_out[layer],
                         w_gate[layer], w_up[layer], w_down[layer])

    xp = x_prompt.reshape(batch * seq, d_model)
    cos_p, sin_p = _rope_tables(jnp.arange(seq, dtype=I32))
    pp = _project(xp, g_attn[layer], w, cos_p, sin_p, 1024, BF16)
    o_sb = _sb_prompt(pp["sb_q"], pp["sb_kv"], batch=batch, seq=seq)
    cmp_p = _compress_prompt(pp["nsa_kv"], w["pe"], w["w1"], w["w2"], batch=batch, seq=seq)
    o_nsa = _nsa_prompt(pp["nsa_q"], cmp_p, pp["nsa_kv"], pp["nsa_win"], pp["nsa_gate"], batch=batch, seq=seq)
    y_prompt = _finish(xp, o_sb, o_nsa, pp["merge"], w, g_ffn[layer], g_final, 1024)

    xs = x_sample.reshape(n_dec, d_model)
    cos_s, sin_s = _rope_tables(jnp.full((n_dec,), past, dtype=I32))
    ps = _project(xs, g_attn[layer], w, cos_s, sin_s, n_dec, F32)
    o_sb_s = _sb_sample(ps["sb_q"].reshape(n_dec, 1, -1), cache_sb_kv, page_table)
    nsa_cache = cache_nsa_kv[layer].reshape(cache_nsa_kv.shape[1], page // CMP_STRIDE, CMP_STRIDE, -1, HEAD_DIM)
    gate_s = ps["nsa_gate"].reshape(n_dec, N_KV_NSA, LANES)[:, :, :3 * NSA_GROUP].reshape(n_dec, N_HEADS_NSA, 3)
    gate_s = jnp.pad(gate_s, ((0, 0), (0, 0), (0, LANES - 3)))
    win_state = state_nsa_win[layer].reshape(n_dec, -1, HEAD_DIM)
    o_nsa_s = _nsa_sample(ps["nsa_q"].reshape(n_dec, N_HEADS_NSA, HEAD_DIM), gate_s,
                          ps["nsa_kv"].reshape(n_dec, 1, -1), win_state, ps["nsa_win"].reshape(n_dec, 1, -1),
                          nsa_cache, page_table, w["pe"], w["w1"], w["w2"])
    y_sample = _finish(xs, o_sb_s.reshape(n_dec, -1).astype(BF16), o_nsa_s.reshape(n_dec, -1).astype(BF16),
                       ps["merge"], w, g_ffn[layer], g_final, n_dec)

    new_win_s = jnp.concatenate([state_nsa_win[layer, :, 1:],
                                 ps["nsa_win"].reshape(n_dec, 1, 2, N_KV_NSA, HEAD_DIM)], axis=1)
    return (
        y_prompt.reshape(batch, seq, d_model),
        y_sample.reshape(n_dec, 1, d_model),
        pp["sb_kv"].reshape(depth, batch, seq, 2, N_HEADS_SB, HEAD_DIM),
        ps["sb_kv"].reshape(depth, n_dec, 1, 2, N_HEADS_SB, HEAD_DIM),
        pp["nsa_kv"].reshape(depth, batch, seq, 4, N_KV_NSA, HEAD_DIM),
        ps["nsa_kv"].reshape(depth, n_dec, 1, 4, N_KV_NSA, HEAD_DIM),
        pp["nsa_win"].reshape(batch, seq, -1)[:, seq - win_len:].reshape(depth, batch, win_len, 2, N_KV_NSA, HEAD_DIM),
        new_win_s[None],
    )
```

```python
import functools

import jax
import jax.numpy as jnp
from jax import lax
from jax.experimental import pallas as pl
from jax.experimental.pallas import tpu as pltpu

F32 = jnp.float32
BF16 = jnp.bfloat16
I32 = jnp.int32

HEAD_DIM = 128
N_HEADS_SB = 16
N_HEADS_NSA = 16
N_KV_NSA = 2
NSA_GROUP = N_HEADS_NSA // N_KV_NSA
CMP_BLOCK = 32
CMP_STRIDE = 16
SEL_BLOCK = 64
SEL_TOP_N = 16
WINDOW = 512
ROPE_THETA = 10000.0
RMS_EPS = 1e-6
NEG_INF = -1e30
BIG = 1e30
LANES = 128
VMEM_LIMIT_BYTES = 48 * 2**20


def _params(*sem):
    return pltpu.CompilerParams(dimension_semantics=sem, vmem_limit_bytes=VMEM_LIMIT_BYTES)


def _dot(a, b):
    return jnp.dot(a, b, preferred_element_type=F32)


def _dot_nt(a, b):
    return lax.dot_general(a, b, (((1,), (1,)), ((), ())), preferred_element_type=F32)


def _lanes(x, n):
    return x if n == LANES else jnp.concatenate([x] * (n // LANES), axis=1)


def _split2(x):
    hi = x.astype(BF16)
    lo = (x - hi.astype(F32)).astype(BF16)
    return hi, lo


def _split3(x):
    hi = x.astype(BF16)
    r1 = x - hi.astype(F32)
    mid = r1.astype(BF16)
    lo = (r1 - mid.astype(F32)).astype(BF16)
    return hi, mid, lo


def _rmsnorm_body(x_ref, g_ref, o_ref):
    x = x_ref[...]
    inv = lax.rsqrt(jnp.mean(x * x, axis=-1, keepdims=True) + RMS_EPS)
    o_ref[...] = (x * inv * g_ref[...]).astype(o_ref.dtype)


def _rmsnorm(x, g, out_dtype, tm):
    m, d = x.shape
    return pl.pallas_call(
        _rmsnorm_body,
        grid=(m // tm,),
        in_specs=[pl.BlockSpec((tm, d), lambda i: (i, 0)), pl.BlockSpec((1, d), lambda i: (0, 0))],
        out_specs=pl.BlockSpec((tm, d), lambda i: (i, 0)),
        out_shape=jax.ShapeDtypeStruct((m, d), out_dtype),
        compiler_params=_params("arbitrary"),
        name="rmsnorm",
    )(x, g.reshape(1, d))


def _rope_tile(y, cos, sin, rope_slabs):
    out = []
    for c, roped in enumerate(rope_slabs):
        x = y[:, c * LANES:(c + 1) * LANES]
        out.append(x * cos + pltpu.roll(x, HEAD_DIM // 2, 1) * sin if roped else x)
    return out[0] if len(out) == 1 else jnp.concatenate(out, axis=1)


def _cast_once(w_ref, wb_ref):
    @pl.when(pl.program_id(1) == 0)
    def _():
        wb_ref[...] = w_ref[...].astype(BF16)


def _mm_body(a_ref, w_ref, o_ref, wb_ref):
    _cast_once(w_ref, wb_ref)
    o_ref[...] = _dot(a_ref[...], wb_ref[...]).astype(o_ref.dtype)


def _mm_rope_body(a_ref, w_ref, cos_ref, sin_ref, o_ref, wb_ref, *, rope_slabs):
    _cast_once(w_ref, wb_ref)
    y = _dot(a_ref[...], wb_ref[...])
    o_ref[...] = _rope_tile(y, cos_ref[...], sin_ref[...], rope_slabs).astype(o_ref.dtype)


def _mm_res_body(a_ref, w_ref, r_ref, o_ref):
    @pl.when(pl.program_id(2) == 0)
    def _():
        o_ref[...] = r_ref[...]

    o_ref[...] += _dot(a_ref[...], w_ref[...])


def _mm_gate_up_body(a_ref, wg_ref, wu_ref, o_ref, wgb_ref, wub_ref):
    _cast_once(wg_ref, wgb_ref)
    _cast_once(wu_ref, wub_ref)
    a = a_ref[...]
    o_ref[...] = (jax.nn.silu(_dot(a, wgb_ref[...])) * _dot(a, wub_ref[...])).astype(o_ref.dtype)


def _mm_merge_body(a1_ref, w1_ref, a2_ref, w2_ref, g1_ref, g2_ref, o_ref):
    y1 = _dot(a1_ref[...], w1_ref[...])
    y2 = _dot(a2_ref[...], w2_ref[...])
    o_ref[...] = (jax.nn.sigmoid(g1_ref[...]) * y1 + jax.nn.sigmoid(g2_ref[...]) * y2).astype(o_ref.dtype)


def _tile(n, pref):
    return pref if n % pref == 0 else n


def _matmul(a, w, out_dtype, *, tm, tn, name, cols=None):
    m, k = a.shape
    c0, c1 = cols or (0, w.shape[1])
    n = c1 - c0
    tm, tn = _tile(m, tm), _tile(n, tn)
    j0 = c0 // tn
    assert j0 * tn == c0
    return pl.pallas_call(
        _mm_body,
        grid=(n // tn, m // tm),
        in_specs=[pl.BlockSpec((tm, k), lambda j, i: (i, 0)), pl.BlockSpec((k, tn), lambda j, i: (0, j0 + j))],
        out_specs=pl.BlockSpec((tm, tn), lambda j, i: (i, j)),
        out_shape=jax.ShapeDtypeStruct((m, n), out_dtype),
        scratch_shapes=[pltpu.VMEM((k, tn), BF16)],
        compiler_params=_params("arbitrary", "arbitrary"),
        name=name,
    )(a, w)


def _matmul_rope(a, w, cos, sin, out_dtype, *, tm, tn, rope_slabs, name, cols):
    m, k = a.shape
    c0, c1 = cols
    n = c1 - c0
    tm, tn = _tile(m, tm), _tile(n, tn)
    j0 = c0 // tn
    assert j0 * tn == c0
    pos_tiles = cos.shape[0] // tm
    return pl.pallas_call(
        functools.partial(_mm_rope_body, rope_slabs=rope_slabs),
        grid=(n // tn, m // tm),
        in_specs=[pl.BlockSpec((tm, k), lambda j, i: (i, 0)), pl.BlockSpec((k, tn), lambda j, i: (0, j0 + j)),
                  pl.BlockSpec((tm, LANES), lambda j, i: (i % pos_tiles, 0)),
                  pl.BlockSpec((tm, LANES), lambda j, i: (i % pos_tiles, 0))],
        out_specs=pl.BlockSpec((tm, tn), lambda j, i: (i, j)),
        out_shape=jax.ShapeDtypeStruct((m, n), out_dtype),
        scratch_shapes=[pltpu.VMEM((k, tn), BF16)],
        compiler_params=_params("arbitrary", "arbitrary"),
        name=name,
    )(a, w, cos, sin)


def _matmul_residual(a, w, res, *, tm, tn, tk, name):
    m, k = a.shape
    n = w.shape[1]
    tm, tn, tk = _tile(m, tm), _tile(n, tn), _tile(k, tk)
    return pl.pallas_call(
        _mm_res_body,
        grid=(m // tm, n // tn, k // tk),
        in_specs=[pl.BlockSpec((tm, tk), lambda i, j, kk: (i, kk)), pl.BlockSpec((tk, tn), lambda i, j, kk: (kk, j)),
                  pl.BlockSpec((tm, tn), lambda i, j, kk: (i, j))],
        out_specs=pl.BlockSpec((tm, tn), lambda i, j, kk: (i, j)),
        out_shape=jax.ShapeDtypeStruct((m, n), F32),
        compiler_params=_params("arbitrary", "arbitrary", "arbitrary"),
        name=name,
    )(a, w, res)


def _matmul_gate_up(a, wg, wu, *, tm, tn, name):
    m, k = a.shape
    n = wg.shape[1]
    tm, tn = _tile(m, tm), _tile(n, tn)
    return pl.pallas_call(
        _mm_gate_up_body,
        grid=(n // tn, m // tm),
        in_specs=[pl.BlockSpec((tm, k), lambda j, i: (i, 0)), pl.BlockSpec((k, tn), lambda j, i: (0, j)),
                  pl.BlockSpec((k, tn), lambda j, i: (0, j))],
        out_specs=pl.BlockSpec((tm, tn), lambda j, i: (i, j)),
        out_shape=jax.ShapeDtypeStruct((m, n), BF16),
        scratch_shapes=[pltpu.VMEM((k, tn), BF16), pltpu.VMEM((k, tn), BF16)],
        compiler_params=_params("arbitrary", "arbitrary"),
        name=name,
    )(a, wg, wu)


def _matmul_merge(a1, w1, a2, w2, gates, *, tm, tn, name):
    m, k = a1.shape
    n = w1.shape[1]
    tm, tn = _tile(m, tm), _tile(n, tn)
    nj = n // tn
    return pl.pallas_call(
        _mm_merge_body,
        grid=(m // tm, nj),
        in_specs=[pl.BlockSpec((tm, k), lambda i, j: (i, 0)), pl.BlockSpec((k, tn), lambda i, j: (0, j)),
                  pl.BlockSpec((tm, k), lambda i, j: (i, 0)), pl.BlockSpec((k, tn), lambda i, j: (0, j)),
                  pl.BlockSpec((tm, tn), lambda i, j: (i, j)), pl.BlockSpec((tm, tn), lambda i, j: (i, j + nj))],
        out_specs=pl.BlockSpec((tm, tn), lambda i, j: (i, j)),
        out_shape=jax.ShapeDtypeStruct((m, n), BF16),
        compiler_params=_params("arbitrary", "arbitrary"),
        name=name,
    )(a1, w1, a2, w2, gates, gates)


def _suffix_matrix(t):
    row = lax.broadcasted_iota(I32, (2 * t, 2 * t), 0) % t
    col = lax.broadcasted_iota(I32, (2 * t, 2 * t), 1)
    return jnp.where((row > col) | (col >= t), 1.0, 0.0).astype(BF16)


def _log_one_minus_beta(z):
    return jnp.minimum(-z, 0.0) - jnp.log(1.0 + jnp.exp(-jnp.abs(z)))


def _sb_block(z, v, u, carry, causal):
    sub = LANES
    n_sub = z.shape[1] // sub
    l1m = _log_one_minus_beta(z)
    if causal is not None:
        l1m = jnp.where(causal, l1m, 0.0)
    hi, lo = _split2(l1m)
    tails = [None] * n_sub
    for j in reversed(range(n_sub)):
        sl = slice(j * sub, (j + 1) * sub)
        sums = _dot(jnp.concatenate([hi[:, sl], lo[:, sl]], axis=1), u)
        tails[j] = sums[:, :sub] + carry
        carry = carry + sums[:, sub:]
    tail = tails[0] if n_sub == 1 else jnp.concatenate(tails, axis=1)
    a = jnp.exp(l1m + z + tail)
    if causal is not None:
        a = jnp.where(causal, a, 0.0)
    return _dot(a.astype(BF16), v), carry


def _sb_prompt_body(q_ref, k_ref, v_ref, o_ref, *, tq, tk, chains, scale):
    seq = q_ref.shape[0]
    rows = tq // chains
    u = _suffix_matrix(LANES)
    col_minus_row = lax.broadcasted_iota(I32, (rows, tk), 1) - lax.broadcasted_iota(I32, (rows, tk), 0)

    def load_kv(kb):
        k0 = pl.multiple_of(kb * tk, tk)
        return k_ref[pl.ds(k0, tk), :].astype(BF16), v_ref[pl.ds(k0, tk), :].astype(BF16)

    def q_tile(qi, _):
        q0 = pl.multiple_of(qi * tq, tq)
        qs = [q_ref[pl.ds(pl.multiple_of(q0 + c * rows, rows), rows), :] for c in range(chains)]
        kb_diag = q0 // tk
        k, v = load_kv(kb_diag)
        state = []
        for c in range(chains):
            causal = col_minus_row < q0 + c * rows - kb_diag * tk
            state += _sb_block(_dot_nt(qs[c], k) * scale, v, u, jnp.zeros((rows, LANES), F32), causal)

        def left(t, st):
            k, v = load_kv(kb_diag - 1 - t)
            new = []
            for c in range(chains):
                pv, carry = _sb_block(_dot_nt(qs[c], k) * scale, v, u, st[2 * c + 1], None)
                new += [st[2 * c] + pv, carry]
            return tuple(new)

        state = lax.fori_loop(0, kb_diag, left, tuple(state))
        for c in range(chains):
            o_ref[pl.ds(pl.multiple_of(q0 + c * rows, rows), rows), :] = state[2 * c].astype(o_ref.dtype)
        return 0

    lax.fori_loop(0, seq // tq, q_tile, 0)


def _sb_prompt(q, kv, *, batch, seq):
    h = N_HEADS_SB
    tq, tk, chains = 4 * LANES, 4 * LANES, 2
    assert tk % tq == 0 and seq % tk == 0
    return pl.pallas_call(
        functools.partial(_sb_prompt_body, tq=tq, tk=tk, chains=chains, scale=HEAD_DIM ** -0.5),
        grid=(batch, h),
        in_specs=[pl.BlockSpec((seq, HEAD_DIM), lambda b, i: (b, i)),
                  pl.BlockSpec((seq, HEAD_DIM), lambda b, i: (b, i)),
                  pl.BlockSpec((seq, HEAD_DIM), lambda b, i: (b, h + i))],
        out_specs=pl.BlockSpec((seq, HEAD_DIM), lambda b, i: (b, i)),
        out_shape=jax.ShapeDtypeStruct(q.shape, BF16),
        compiler_params=_params("arbitrary", "arbitrary"),
        name="sb_prompt",
    )(q, kv, kv)


SB_PAGES_PER_STEP = 8
SUBLANES = 8


def _sb_sample_body(pt_ref, q_ref, cache_ref, o_ref, buf_ref, sem_ref, acc_ref, carry_ref, *, n_pages, scale):
    b, pp = pl.program_id(0), pl.program_id(1)
    steps = pl.num_programs(1)
    t = b * steps + pp
    slot = t % 2
    d = N_HEADS_SB * HEAD_DIM
    page = cache_ref.shape[2]
    halves = N_HEADS_SB // SUBLANES
    per_page = 2 * halves

    def copies(seq, step, to_slot):
        out = []
        for j in range(SB_PAGES_PER_STEP):
            pg = pt_ref[seq, n_pages - 1 - (step * SB_PAGES_PER_STEP + j)]
            for kv in range(2):
                for half in range(halves):
                    out.append(pltpu.make_async_copy(
                        cache_ref.at[0, pg, :, kv, pl.ds(half * SUBLANES, SUBLANES), :],
                        buf_ref.at[to_slot, j * per_page + kv * halves + half], sem_ref.at[to_slot]))
        return out

    @pl.when(t == 0)
    def _():
        for c in copies(0, 0, 0):
            c.start()

    @pl.when(t + 1 < pl.num_programs(0) * steps)
    def _():
        for c in copies((t + 1) // steps, (t + 1) % steps, 1 - slot):
            c.start()

    for c in copies(b, pp, slot):
        c.wait()

    @pl.when(pp == 0)
    def _():
        acc_ref[...] = jnp.zeros_like(acc_ref)
        carry_ref[...] = jnp.zeros_like(carry_ref)

    head = lax.broadcasted_iota(I32, (N_HEADS_SB, d), 0)
    lane_head = lax.broadcasted_iota(I32, (N_HEADS_SB, d), 1) // HEAD_DIM
    own = head == lane_head
    q_bd = jnp.where(own, jnp.broadcast_to(q_ref[0], (N_HEADS_SB, d)), 0.0).astype(BF16)
    u = _suffix_matrix(LANES)

    block_rows = page * SUBLANES
    flat = buf_ref.reshape(2 * SB_PAGES_PER_STEP * per_page * block_rows, HEAD_DIM)

    def heads_to_lanes(first_block):
        base = (slot * SB_PAGES_PER_STEP * per_page + first_block) * block_rows
        return jnp.concatenate([flat[pl.ds(base + i * block_rows + h, page, stride=SUBLANES), :]
                                for i in range(halves) for h in range(SUBLANES)], axis=1).astype(BF16)

    acc, carry = acc_ref[...], carry_ref[...]
    for j in range(SB_PAGES_PER_STEP):
        k = heads_to_lanes(j * per_page)
        v = heads_to_lanes(j * per_page + halves)
        pv, carry = _sb_block(_dot_nt(q_bd, k) * scale, v, u, carry, None)
        acc = acc + pv
    acc_ref[...] = acc
    carry_ref[...] = carry

    @pl.when(pp == pl.num_programs(1) - 1)
    def _():
        o_ref[0] = jnp.sum(jnp.where(own, acc, 0.0), axis=0, keepdims=True)


def _sb_sample(q, cache, page_table):
    s, n_pages = page_table.shape
    d = N_HEADS_SB * HEAD_DIM
    page = cache.shape[2]
    assert page == LANES and n_pages % SB_PAGES_PER_STEP == 0
    steps = n_pages // SB_PAGES_PER_STEP
    n_blocks = SB_PAGES_PER_STEP * 2 * N_HEADS_SB // SUBLANES
    grid_spec = pltpu.PrefetchScalarGridSpec(
        num_scalar_prefetch=1,
        grid=(s, steps),
        in_specs=[pl.BlockSpec((1, 1, d), lambda b, p, pt: (b, 0, 0)), pl.BlockSpec(memory_space=pl.ANY)],
        out_specs=pl.BlockSpec((1, 1, d), lambda b, p, pt: (b, 0, 0)),
        scratch_shapes=[pltpu.VMEM((2, n_blocks, page, SUBLANES, HEAD_DIM), F32), pltpu.SemaphoreType.DMA((2,)),
                        pltpu.VMEM((N_HEADS_SB, d), F32), pltpu.VMEM((N_HEADS_SB, LANES), F32)],
    )
    return pl.pallas_call(
        functools.partial(_sb_sample_body, n_pages=n_pages, scale=HEAD_DIM ** -0.5),
        grid_spec=grid_spec,
        out_shape=jax.ShapeDtypeStruct((s, 1, d), F32),
        compiler_params=_params("arbitrary", "arbitrary"),
        name="sb_sample",
    )(page_table, q, cache)


def _compress_rows(x, pe_ref, w1_ref, w2_ref, kind, n_cmp):
    c = LANES
    pa = _dot((x + pe_ref[kind, 0]).astype(BF16), w1_ref[kind, 0])
    pb = _dot((x + pe_ref[kind, 1]).astype(BF16), w1_ref[kind, 1])
    pb = jnp.concatenate([pltpu.roll(pb[i:i + c], c - 1, 0) for i in range(0, x.shape[0], c)], axis=0)
    out = _dot(jax.nn.gelu(pa + pb).astype(BF16), w2_ref[kind])
    row = lax.broadcasted_iota(I32, out.shape, 0) % c
    return jnp.where(row < n_cmp, out, 0.0)


def _compress_prompt_body(*refs, n_cmp):
    rows, (pe_ref, w1_ref, w2_ref, o_ref) = refs[:2 * N_KV_NSA], refs[2 * N_KV_NSA:]
    chunks = rows[0].shape[0] // CMP_STRIDE
    for kind in range(2):
        xs = [jnp.concatenate([rows[kind * N_KV_NSA + g][pl.ds(l, chunks, stride=CMP_STRIDE), :]
                               for l in range(CMP_STRIDE)], axis=1) for g in range(N_KV_NSA)]
        out = _compress_rows(jnp.concatenate(xs, axis=0), pe_ref, w1_ref, w2_ref, kind, n_cmp)
        for g in range(N_KV_NSA):
            o_ref[0, kind, g] = out[g * LANES:(g + 1) * LANES]


def _compress_prompt(nkv, pe, w1, w2, *, batch, seq):
    chunks = seq // CMP_STRIDE
    assert chunks == LANES
    n_cmp = (seq - CMP_BLOCK) // CMP_STRIDE + 1
    col = lambda c: pl.BlockSpec((seq, HEAD_DIM), lambda b, c=c: (b, c))
    return pl.pallas_call(
        functools.partial(_compress_prompt_body, n_cmp=n_cmp),
        grid=(batch,),
        in_specs=[col(c) for c in range(2 * N_KV_NSA)]
        + [pl.BlockSpec(pe.shape, lambda b: (0, 0, 0, 0)),
           pl.BlockSpec(w1.shape, lambda b: (0, 0, 0, 0)),
           pl.BlockSpec(w2.shape, lambda b: (0, 0, 0))],
        out_specs=pl.BlockSpec((1, 2, N_KV_NSA, chunks, HEAD_DIM), lambda b: (b, 0, 0, 0, 0)),
        out_shape=jax.ShapeDtypeStruct((batch, 2, N_KV_NSA, chunks, HEAD_DIM), F32),
        compiler_params=_params("arbitrary"),
        name="compress_prompt",
    )(*([nkv] * (2 * N_KV_NSA)), pe, w1, w2)


def _overlap(n_idx, m_idx, n_cmp, n_sel):
    d = n_idx * CMP_STRIDE - m_idx * SEL_BLOCK
    d = jnp.where(n_idx < n_cmp, d, SEL_BLOCK)
    d = jnp.where(m_idx < n_sel, d, SEL_BLOCK)
    return jnp.where(d > -CMP_BLOCK, d, SEL_BLOCK) < SEL_BLOCK


def _force_blocks(imp, m_idx, cur):
    forced = (m_idx == 0) | (m_idx == cur) | (m_idx == cur - 1)
    imp = jnp.where(forced, BIG, imp)
    return jnp.where(m_idx <= cur, imp, -BIG)


LOG2_E = 1.4426950408889634


def _attend(qs, k_ref, v_ref, lo, hi, tk, mask_fn, acc_ref, m_ref, scale, first_tile_hits_every_row, chains):
    tq = m_ref.shape[0] // NSA_GROUP
    per_chain = NSA_GROUP // chains
    c2 = scale * LOG2_E
    acc_ref[...] = jnp.zeros_like(acc_ref)
    m_ref[...] = jnp.full(m_ref.shape, NEG_INF, F32)
    ones = jnp.ones((tk, LANES), BF16)

    def step(kt, c):
        k0 = pl.multiple_of(kt * tk, tk)
        kb = k_ref[pl.ds(k0, tk), :].astype(BF16)
        vb = jnp.concatenate([v_ref[pl.ds(k0, tk), :].astype(BF16), ones], axis=1)
        mask = mask_fn(k0)
        for chain in range(chains):
            base = chain * per_chain * tq
            s = _dot_nt(qs[base:base + per_chain * tq], kb)
            ps, alphas = [], []
            for r in range(per_chain):
                sl = slice(base + r * tq, base + (r + 1) * tq)
                s_r = jnp.where(mask, s[r * tq:(r + 1) * tq], NEG_INF)
                m_prev = m_ref[sl, :]
                m_new = jnp.maximum(m_prev, jnp.max(s_r, axis=-1, keepdims=True))
                alphas.append(jnp.exp2((m_prev - m_new) * c2))
                m_ref[sl, :] = m_new
                p = jnp.exp2((s_r - _lanes(m_new, tk)) * c2)
                if not first_tile_hits_every_row:
                    p = jnp.where(mask, p, 0.0)
                ps.append(p.astype(BF16))
            pv = _dot(jnp.concatenate(ps, axis=0), vb)
            for r in range(per_chain):
                sl = slice(base + r * tq, base + (r + 1) * tq)
                acc_ref[sl, :] = acc_ref[sl, :] * _lanes(alphas[r], 2 * LANES) + pv[r * tq:(r + 1) * tq]
        return c

    lax.fori_loop(lo, hi, step, 0)
    acc = acc_ref[...]
    return acc[:, :LANES] / jnp.maximum(acc[:, LANES:], 1e-30)


def _nsa_prompt_body(q_ref, kc_ref, vc_ref, ks_ref, vs_ref, kw_ref, vw_ref, gate_ref, o_ref, acc_ref, m_ref,
                     *, n_cmp, scale):
    tq = q_ref.shape[0]
    seq = ks_ref.shape[0]
    n_sel = seq // SEL_BLOCK
    qi = pl.program_id(2)
    q0 = qi * tq
    q = q_ref[...]
    qs = jnp.concatenate([q[:, r * HEAD_DIM:(r + 1) * HEAD_DIM] for r in range(NSA_GROUP)], axis=0)
    kc = kc_ref[0, 0, 0].astype(BF16)

    n_idx = lax.broadcasted_iota(I32, (LANES, tq), 0)
    qpos_l = q0 + lax.broadcasted_iota(I32, (LANES, tq), 1)
    vis_t = jnp.where(n_idx < n_cmp, n_idx * CMP_STRIDE + CMP_BLOCK - 1, seq) <= qpos_l
    s_t = _dot_nt(kc, qs)
    p_sum = jnp.zeros((LANES, tq), F32)
    for r in range(NSA_GROUP):
        s_r = jnp.where(vis_t, s_t[:, r * tq:(r + 1) * tq] * scale, NEG_INF)
        e = jnp.where(vis_t, jnp.exp(s_r - jnp.max(s_r, axis=0, keepdims=True)), 0.0)
        p_sum = p_sum + e / jnp.maximum(jnp.sum(e, axis=0, keepdims=True), 1e-30)
    m_row = lax.broadcasted_iota(I32, (LANES, LANES), 0)
    n_col = lax.broadcasted_iota(I32, (LANES, LANES), 1)
    ov_t = jnp.where(_overlap(n_col, m_row, n_cmp, n_sel), 1.0, 0.0).astype(BF16)
    imp = sum(_dot(ov_t, part) for part in _split3(p_sum))[:n_sel]
    m_idx = lax.broadcasted_iota(I32, (n_sel, tq), 0)
    cur = (q0 + lax.broadcasted_iota(I32, (n_sel, tq), 1)) // SEL_BLOCK
    imp = _force_blocks(imp, m_idx, cur)
    rank = jnp.zeros((n_sel, tq), F32)
    for mp in range(n_sel):
        other = imp[mp:mp + 1, :]
        tie = jnp.where(m_idx > mp, 1.0, 0.0)
        rank = rank + jnp.where(other > imp, 1.0, jnp.where(other == imp, tie, 0.0))
    sel = jnp.where(rank < SEL_TOP_N, 1.0, 0.0)
    sel_t = jnp.concatenate([sel, jnp.zeros((LANES - n_sel, tq), F32)], axis=0).T.astype(BF16)

    def cmp_mask(k0):
        n = lax.broadcasted_iota(I32, (tq, LANES), 1)
        qpos = q0 + lax.broadcasted_iota(I32, (tq, LANES), 0)
        return jnp.where(n < n_cmp, n * CMP_STRIDE + CMP_BLOCK - 1, seq) <= qpos

    tk_s = 4 * LANES

    def sel_mask(k0):
        blk = (k0 + lax.broadcasted_iota(I32, (LANES, tk_s), 1)) // SEL_BLOCK
        expand = jnp.where(blk == lax.broadcasted_iota(I32, (LANES, tk_s), 0), 1.0, 0.0).astype(BF16)
        chosen = _dot(sel_t, expand)
        kpos = k0 + lax.broadcasted_iota(I32, (tq, tk_s), 1)
        qpos = q0 + lax.broadcasted_iota(I32, (tq, tk_s), 0)
        return jnp.where(kpos <= qpos, chosen, 0.0) > 0.5

    tk_w = 2 * LANES

    def win_mask(k0):
        kpos = k0 + lax.broadcasted_iota(I32, (tq, tk_w), 1)
        qpos = q0 + lax.broadcasted_iota(I32, (tq, tk_w), 0)
        back = qpos - kpos
        return jnp.where(back >= 0, back, WINDOW + 1) <= WINDOW

    o_c = _attend(qs, kc_ref.at[0, 0, 0], vc_ref.at[0, 0, 0], 0, 1, LANES, cmp_mask, acc_ref, m_ref, scale, False, 1)
    o_s = _attend(qs, ks_ref, vs_ref, 0, (q0 + tq + tk_s - 1) // tk_s, tk_s, sel_mask, acc_ref, m_ref, scale, True, 2)
    first_w = jnp.maximum(q0 - WINDOW, 0) // tk_w
    o_w = _attend(qs, kw_ref, vw_ref, first_w, (q0 + tq + tk_w - 1) // tk_w, tk_w, win_mask, acc_ref, m_ref, scale,
                  True, 2)

    gate = jax.nn.sigmoid(gate_ref[...])
    for r in range(NSA_GROUP):
        sl = slice(r * tq, (r + 1) * tq)
        o = (gate[:, 3 * r:3 * r + 1] * o_c[sl] + gate[:, 3 * r + 1:3 * r + 2] * o_s[sl]
             + gate[:, 3 * r + 2:3 * r + 3] * o_w[sl])
        o_ref[:, r * HEAD_DIM:(r + 1) * HEAD_DIM] = o.astype(o_ref.dtype)


def _nsa_prompt(q, cmp, nkv, win, gates, *, batch, seq):
    tq = LANES
    assert tq == LANES and seq % (4 * LANES) == 0
    nq = seq // tq
    gw = NSA_GROUP * HEAD_DIM
    n_cmp = (seq - CMP_BLOCK) // CMP_STRIDE + 1
    rows = lambda c: pl.BlockSpec((seq, HEAD_DIM), lambda b, g, i, c=c: (b, c * N_KV_NSA + g))
    cmp_spec = lambda kind: pl.BlockSpec((1, 1, 1, LANES, HEAD_DIM), lambda b, g, i, kind=kind: (b, kind, g, 0, 0))
    return pl.pallas_call(
        functools.partial(_nsa_prompt_body, n_cmp=n_cmp, scale=HEAD_DIM ** -0.5),
        grid=(batch, N_KV_NSA, nq),
        in_specs=[pl.BlockSpec((tq, gw), lambda b, g, i: (b * nq + i, g)),
                  cmp_spec(0), cmp_spec(1), rows(2), rows(3), rows(0), rows(1),
                  pl.BlockSpec((tq, LANES), lambda b, g, i: (b * nq + i, g))],
        out_specs=pl.BlockSpec((tq, gw), lambda b, g, i: (b * nq + i, g)),
        out_shape=jax.ShapeDtypeStruct(q.shape, BF16),
        scratch_shapes=[pltpu.VMEM((NSA_GROUP * tq, 2 * LANES), F32), pltpu.VMEM((NSA_GROUP * tq, LANES), F32)],
        compiler_params=_params("arbitrary", "arbitrary", "arbitrary"),
        name="nsa_prompt",
    )(q, cmp, cmp, nkv, nkv, win, win, gates)


def _softmax_rows(s, vis):
    s = jnp.where(vis, s, NEG_INF)
    e = jnp.where(vis, jnp.exp(s - jnp.max(s, axis=-1, keepdims=True)), 0.0)
    return e / jnp.maximum(jnp.sum(e, axis=-1, keepdims=True), 1e-30)


def _nsa_sample_body(pt_ref, cache_ref, new_ref, wst_ref, wnew_ref, q_ref, gate_ref, pe_ref, w1_ref, w2_ref,
                     o_ref, buf_ref, sem_ref, *, n_pages, past, scale):
    b = pl.program_id(0)
    slot = b % 2
    slabs = 4 * N_KV_NSA
    page_chunks = cache_ref.shape[1]
    chunks = n_pages * page_chunks
    pos = past
    n_keys = past + 1
    n_cmp = (n_keys - CMP_BLOCK) // CMP_STRIDE + 1
    n_sel = -(-n_keys // SEL_BLOCK)
    w_buf = wst_ref.shape[1] // (2 * N_KV_NSA)
    gd = N_KV_NSA * HEAD_DIM
    r8 = NSA_GROUP

    def copies(seq, to_slot):
        out = []
        for p in range(n_pages):
            page = pt_ref[seq, p]
            for l in range(CMP_STRIDE):
                out.append(pltpu.make_async_copy(cache_ref.at[page, :, l],
                                                 buf_ref.at[to_slot, l, pl.ds(p * page_chunks, page_chunks)],
                                                 sem_ref.at[to_slot]))
        return out

    @pl.when(b == 0)
    def _():
        for c in copies(0, 0):
            c.start()

    @pl.when(b + 1 < pl.num_programs(0))
    def _():
        for c in copies(b + 1, 1 - slot):
            c.start()

    for c in copies(b, slot):
        c.wait()

    flat = buf_ref.reshape(2 * CMP_STRIDE * chunks * slabs, HEAD_DIM)

    def rows_of(l, slab):
        start = (slot * CMP_STRIDE + l) * (chunks * slabs) + slab
        return flat[pl.ds(start, chunks, stride=slabs), :]

    cmp = []
    for kind in range(2):
        xs = [jnp.concatenate([rows_of(l, kind * N_KV_NSA + g) for l in range(CMP_STRIDE)], axis=1)
              for g in range(N_KV_NSA)]
        cmp.append(_compress_rows(jnp.concatenate(xs, axis=0), pe_ref, w1_ref, w2_ref, kind, n_cmp))

    lane = lax.broadcasted_iota(I32, (r8, LANES), 1)
    sq_row = lax.broadcasted_iota(I32, (LANES, LANES), 0)
    sq_col = lax.broadcasted_iota(I32, (LANES, LANES), 1)
    ov = jnp.where(_overlap(sq_row, sq_col, n_cmp, n_sel), 1.0, 0.0).astype(BF16)
    lower_first = jnp.where(sq_row < sq_col, 1.0, 0.0)
    expand = jnp.where(sq_col // (SEL_BLOCK // CMP_STRIDE) == sq_row, 1.0, 0.0).astype(BF16)
    widx = lax.broadcasted_iota(I32, (r8, w_buf + LANES), 1)
    gate = jax.nn.sigmoid(gate_ref[0])

    def with_new(past_rows, new_row):
        return jnp.concatenate([past_rows, jnp.broadcast_to(new_row, (LANES, HEAD_DIM))], axis=0).astype(BF16)

    for g in range(N_KV_NSA):
        q = q_ref[0, g * r8:(g + 1) * r8, :].astype(BF16)
        kc = cmp[0][g * LANES:(g + 1) * LANES].astype(BF16)
        vc = cmp[1][g * LANES:(g + 1) * LANES].astype(BF16)
        vis_c = jnp.where(lane < n_cmp, lane * CMP_STRIDE + CMP_BLOCK - 1, pos + 1) <= pos
        p_c = _softmax_rows(_dot_nt(q, kc) * scale, vis_c)
        o_c = _dot(p_c.astype(BF16), vc)

        p_sum = jnp.broadcast_to(jnp.sum(p_c, axis=0, keepdims=True), (r8, LANES))
        imp = sum(_dot(part, ov) for part in _split3(p_sum))
        imp = _force_blocks(imp, lane, pos // SEL_BLOCK)
        imp_b = jnp.broadcast_to(imp[0:1], (LANES, LANES))
        imp_c = imp_b.T
        beats = jnp.where(imp_c > imp_b, 1.0, jnp.where(imp_c == imp_b, lower_first, 0.0))
        rank = jnp.sum(beats, axis=0, keepdims=True)
        sel = jnp.broadcast_to(jnp.where(rank < SEL_TOP_N, 1.0, 0.0), (r8, LANES))
        cur = pos // SEL_BLOCK
        new_tile = jnp.where(lane == 0, sel[:, cur:cur + 1], 0.0)
        vis_s = jnp.concatenate([_dot(sel.astype(BF16), expand)] * CMP_STRIDE + [new_tile], axis=1) > 0.5

        c_k = (2 * N_KV_NSA + g) * HEAD_DIM
        c_v = (3 * N_KV_NSA + g) * HEAD_DIM
        past_rows = lambda slab: jnp.concatenate([rows_of(l, slab) for l in range(CMP_STRIDE)], axis=0)
        k_s = with_new(past_rows(2 * N_KV_NSA + g), new_ref[0, :, c_k:c_k + HEAD_DIM])
        v_s = with_new(past_rows(3 * N_KV_NSA + g), new_ref[0, :, c_v:c_v + HEAD_DIM])
        p_s = _softmax_rows(_dot_nt(q, k_s) * scale, vis_s)
        o_s = _dot(p_s.astype(BF16), v_s)

        win_rows = lambda kv: wst_ref[0, pl.ds(kv * N_KV_NSA + g, w_buf, stride=2 * N_KV_NSA), :]
        k_w = with_new(win_rows(0), wnew_ref[0, :, g * HEAD_DIM:(g + 1) * HEAD_DIM])
        v_w = with_new(win_rows(1), wnew_ref[0, :, gd + g * HEAD_DIM:gd + (g + 1) * HEAD_DIM])
        p_w = _softmax_rows(_dot_nt(q, k_w) * scale, widx <= w_buf)
        o_w = _dot(p_w.astype(BF16), v_w)

        gt = gate[g * r8:(g + 1) * r8]
        o_ref[0, g * r8:(g + 1) * r8, :] = gt[:, 0:1] * o_c + gt[:, 1:2] * o_s + gt[:, 2:3] * o_w


def _nsa_sample(q, gates, new_nkv, win_state, new_win, cache, page_table, pe, w1, w2):
    s, n_pages = page_table.shape
    page_chunks, slabs = cache.shape[1], cache.shape[3]
    chunks = n_pages * page_chunks
    past = chunks * CMP_STRIDE
    assert chunks == LANES and cache.shape[2] == CMP_STRIDE and win_state.shape[1] <= WINDOW * 2 * N_KV_NSA
    per_seq = lambda a: pl.BlockSpec((1,) + a.shape[1:], lambda b, pt: (b, 0, 0))
    whole = lambda a: pl.BlockSpec(a.shape, lambda b, pt, n=a.ndim: (0,) * n)
    grid_spec = pltpu.PrefetchScalarGridSpec(
        num_scalar_prefetch=1,
        grid=(s,),
        in_specs=[pl.BlockSpec(memory_space=pl.ANY),
                  per_seq(new_nkv), per_seq(win_state), per_seq(new_win), per_seq(q), per_seq(gates),
                  whole(pe), whole(w1), whole(w2)],
        out_specs=pl.BlockSpec((1, N_HEADS_NSA, HEAD_DIM), lambda b, pt: (b, 0, 0)),
        scratch_shapes=[pltpu.VMEM((2, CMP_STRIDE, chunks, slabs, HEAD_DIM), F32), pltpu.SemaphoreType.DMA((2,))],
    )
    return pl.pallas_call(
        functools.partial(_nsa_sample_body, n_pages=n_pages, past=past, scale=HEAD_DIM ** -0.5),
        grid_spec=grid_spec,
        out_shape=jax.ShapeDtypeStruct((s, N_HEADS_NSA, HEAD_DIM), F32),
        compiler_params=_params("arbitrary"),
        name="nsa_sample",
    )(page_table, cache, new_nkv, win_state, new_win, q, gates, pe, w1, w2)


def _rope_tables(pos):
    half = HEAD_DIM // 2
    inv_freq = ROPE_THETA ** (-jnp.arange(half, dtype=F32) / half)
    ang = pos.astype(F32)[:, None] * inv_freq[None, :]
    cos, sin = jnp.cos(ang), jnp.sin(ang)
    return jnp.concatenate([cos, cos], axis=1), jnp.concatenate([-sin, sin], axis=1)


def _prepare_weights(w_in, cmp_k_pe, cmp_k_w1, cmp_k_w2, cmp_v_pe, cmp_v_w1, cmp_v_w2,
                     w_branch_sb, w_branch_nsa, w_out, w_gate, w_up, w_down):
    d_sb = N_HEADS_SB * HEAD_DIM
    d_nsa = N_HEADS_NSA * HEAD_DIM
    d_kv = N_KV_NSA * HEAD_DIM
    cuts = [0, d_sb, 3 * d_sb, 3 * d_sb + d_nsa, 3 * d_sb + d_nsa + 4 * d_kv, 3 * d_sb + d_nsa + 6 * d_kv]
    n_gate = 3 * N_HEADS_NSA
    w_ng = w_in[:, cuts[5]:cuts[5] + n_gate].reshape(-1, N_KV_NSA, 3 * NSA_GROUP)
    w_ng = jnp.pad(w_ng, ((0, 0), (0, 0), (0, LANES - 3 * NSA_GROUP))).reshape(-1, N_KV_NSA * LANES)
    half = CMP_STRIDE * HEAD_DIM
    halves = lambda w1: w1.reshape(2, half, -1)
    return dict(
        w_in=w_in, cuts=dict(sb_q=(cuts[0], cuts[1]), sb_kv=(cuts[1], cuts[2]), nsa_q=(cuts[2], cuts[3]),
                             nsa_kv=(cuts[3], cuts[4]), nsa_win=(cuts[4], cuts[5])),
        nsa_gate=w_ng, merge=w_in[:, cuts[5] + n_gate:],
        pe=jnp.stack([cmp_k_pe.reshape(2, 1, half), cmp_v_pe.reshape(2, 1, half)]),
        w1=jnp.stack([halves(cmp_k_w1), halves(cmp_v_w1)]).astype(BF16),
        w2=jnp.stack([cmp_k_w2, cmp_v_w2]).astype(BF16),
        branch_sb=w_branch_sb.astype(BF16), branch_nsa=w_branch_nsa.astype(BF16), out=w_out.astype(BF16),
        gate=w_gate, up=w_up, down=w_down.astype(BF16),
    )


def _project(x, g_attn, w, cos, sin, tm, q_dtype):
    h = _rmsnorm(x, g_attn, BF16, min(tm, 256))
    mm = functools.partial(_matmul, h, tm=tm)
    seg = lambda s, dt: mm(w["w_in"], dt, tn=512, cols=w["cuts"][s], name="proj_" + s)
    rope = lambda s, dt, slabs: _matmul_rope(h, w["w_in"], cos, sin, dt, tm=tm, tn=4 * LANES, rope_slabs=slabs,
                                             cols=w["cuts"][s], name="proj_" + s)
    k_then_v = (True, True, False, False)
    return dict(
        sb_q=seg("sb_q", q_dtype), sb_kv=seg("sb_kv", F32),
        nsa_q=rope("nsa_q", q_dtype, (True,) * 4), nsa_kv=rope("nsa_kv", F32, k_then_v),
        nsa_win=rope("nsa_win", F32, k_then_v),
        nsa_gate=mm(w["nsa_gate"], F32, tn=256, name="proj_nsa_gate"),
        merge=mm(w["merge"], F32, tn=512, name="proj_merge"),
    )


def _finish(x, o_sb, o_nsa, merge, w, g_ffn, g_final, tm):
    merged = _matmul_merge(o_sb, w["branch_sb"], o_nsa, w["branch_nsa"], merge, tm=tm, tn=512, name="branch_merge")
    x = _matmul_residual(merged, w["out"], x, tm=tm, tn=512, tk=merged.shape[1], name="out_proj")
    h = _rmsnorm(x, g_ffn, BF16, min(tm, 256))
    ff = _matmul_gate_up(h, w["gate"], w["up"], tm=tm, tn=256, name="ffn_gate_up")
    x = _matmul_residual(ff, w["down"], x, tm=tm, tn=512, tk=ff.shape[1] // 2, name="ffn_down")
    return _rmsnorm(x, g_final, F32, min(tm, 256))


def kernel(x_prompt, x_sample, cache_sb_kv, cache_nsa_kv, state_nsa_win, page_table, g_attn, w_in, cmp_k_pe, cmp_k_w1, cmp_k_w2, cmp_v_pe, cmp_v_w1, cmp_v_w2, w_branch_sb, w_branch_nsa, w_out, g_ffn, w_gate, w_up, w_down, g_final):
    depth = w_in.shape[0]
    assert depth == 1
    batch, seq, d_model = x_prompt.shape
    n_dec = x_sample.shape[0]
    assert x_sample.shape[1] == 1
    n_pages = page_table.shape[1]
    page = cache_sb_kv.shape[2]
    past = n_pages * page
    win_len = min(WINDOW, seq)
    layer = 0
    w = _prepare_weights(w_in[layer], cmp_k_pe[layer], cmp_k_w1[layer], cmp_k_w2[layer], cmp_v_pe[layer],
                         cmp_v_w1[layer], cmp_v_w2[layer], w_branch_sb[layer], w_branch_nsa[layer], w_out[layer],
                         w_gate[layer], w_up[layer], w_down[layer])

    xp = x_prompt.reshape(batch * seq, d_model)
    cos_p, sin_p = _rope_tables(jnp.arange(seq, dtype=I32))
    pp = _project(xp, g_attn[layer], w, cos_p, sin_p, 1024, BF16)
    o_sb = _sb_prompt(pp["sb_q"], pp["sb_kv"], batch=batch, seq=seq)
    cmp_p = _compress_prompt(pp["nsa_kv"], w["pe"], w["w1"], w["w2"], batch=batch, seq=seq)
    o_nsa = _nsa_prompt(pp["nsa_q"], cmp_p, pp["nsa_kv"], pp["nsa_win"], pp["nsa_gate"], batch=batch, seq=seq)
    y_prompt = _finish(xp, o_sb, o_nsa, pp["merge"], w, g_ffn[layer], g_final, 1024)

    xs = x_sample.reshape(n_dec, d_model)
    cos_s, sin_s = _rope_tables(jnp.full((n_dec,), past, dtype=I32))
    ps = _project(xs, g_attn[layer], w, cos_s, sin_s, n_dec, F32)
    o_sb_s = _sb_sample(ps["sb_q"].reshape(n_dec, 1, -1), cache_sb_kv, page_table)
    nsa_cache = cache_nsa_kv[layer].reshape(cache_nsa_kv.shape[1], page // CMP_STRIDE, CMP_STRIDE, -1, HEAD_DIM)
    gate_s = ps["nsa_gate"].reshape(n_dec, N_KV_NSA, LANES)[:, :, :3 * NSA_GROUP].reshape(n_dec, N_HEADS_NSA, 3)
    gate_s = jnp.pad(gate_s, ((0, 0), (0, 0), (0, LANES - 3)))
    win_state = state_nsa_win[layer].reshape(n_dec, -1, HEAD_DIM)
    o_nsa_s = _nsa_sample(ps["nsa_q"].reshape(n_dec, N_HEADS_NSA, HEAD_DIM), gate_s,
                          ps["nsa_kv"].reshape(n_dec, 1, -1), win_state, ps["nsa_win"].reshape(n_dec, 1, -1),
                          nsa_cache, page_table, w["pe"], w["w1"], w["w2"])
    y_sample = _finish(xs, o_sb_s.reshape(n_dec, -1).astype(BF16), o_nsa_s.reshape(n_dec, -1).astype(BF16),
                       ps["merge"], w, g_ffn[layer], g_final, n_dec)

    new_win_s = jnp.concatenate([state_nsa_win[layer, :, 1:],
                                 ps["nsa_win"].reshape(n_dec, 1, 2, N_KV_NSA, HEAD_DIM)], axis=1)
    return (
        y_prompt.reshape(batch, seq, d_model),
        y_sample.reshape(n_dec, 1, d_model),
        pp["sb_kv"].reshape(depth, batch, seq, 2, N_HEADS_SB, HEAD_DIM),
        ps["sb_kv"].reshape(depth, n_dec, 1, 2, N_HEADS_SB, HEAD_DIM),
        pp["nsa_kv"].reshape(depth, batch, seq, 4, N_KV_NSA, HEAD_DIM),
        ps["nsa_kv"].reshape(depth, n_dec, 1, 4, N_KV_NSA, HEAD_DIM),
        pp["nsa_win"].reshape(batch, seq, -1)[:, seq - win_len:].reshape(depth, batch, win_len, 2, N_KV_NSA, HEAD_DIM),
        new_win_s[None],
    )
```

```python
import functools

import jax
import jax.numpy as jnp
from jax import lax
from jax.experimental import pallas as pl
from jax.experimental.pallas import tpu as pltpu

F32 = jnp.float32
BF16 = jnp.bfloat16
I32 = jnp.int32

HEAD_DIM = 128
N_HEADS_SB = 16
N_HEADS_NSA = 16
N_KV_NSA = 2
NSA_GROUP = N_HEADS_NSA // N_KV_NSA
CMP_BLOCK = 32
CMP_STRIDE = 16
SEL_BLOCK = 64
SEL_TOP_N = 16
WINDOW = 512
ROPE_THETA = 10000.0
RMS_EPS = 1e-6
NEG_INF = -1e30
BIG = 1e30
LANES = 128
VMEM_LIMIT_BYTES = 48 * 2**20


def _params(*sem):
    return pltpu.CompilerParams(dimension_semantics=sem, vmem_limit_bytes=VMEM_LIMIT_BYTES)


def _dot(a, b):
    return jnp.dot(a, b, preferred_element_type=F32)


def _dot_nt(a, b):
    return lax.dot_general(a, b, (((1,), (1,)), ((), ())), preferred_element_type=F32)


def _lanes(x, n):
    return x if n == LANES else jnp.concatenate([x] * (n // LANES), axis=1)


def _split2(x):
    hi = x.astype(BF16)
    lo = (x - hi.astype(F32)).astype(BF16)
    return hi, lo


def _split3(x):
    hi = x.astype(BF16)
    r1 = x - hi.astype(F32)
    mid = r1.astype(BF16)
    lo = (r1 - mid.astype(F32)).astype(BF16)
    return hi, mid, lo


def _rmsnorm_body(x_ref, g_ref, o_ref):
    x = x_ref[...]
    inv = lax.rsqrt(jnp.mean(x * x, axis=-1, keepdims=True) + RMS_EPS)
    o_ref[...] = (x * inv * g_ref[...]).astype(o_ref.dtype)


def _rmsnorm(x, g, out_dtype, tm):
    m, d = x.shape
    return pl.pallas_call(
        _rmsnorm_body,
        grid=(m // tm,),
        in_specs=[pl.BlockSpec((tm, d), lambda i: (i, 0)), pl.BlockSpec((1, d), lambda i: (0, 0))],
        out_specs=pl.BlockSpec((tm, d), lambda i: (i, 0)),
        out_shape=jax.ShapeDtypeStruct((m, d), out_dtype),
        compiler_params=_params("arbitrary"),
        name="rmsnorm",
    )(x, g.reshape(1, d))


def _rope_tile(y, cos, sin, rope_slabs):
    out = []
    for c, roped in enumerate(rope_slabs):
        x = y[:, c * LANES:(c + 1) * LANES]
        out.append(x * cos + pltpu.roll(x, HEAD_DIM // 2, 1) * sin if roped else x)
    return out[0] if len(out) == 1 else jnp.concatenate(out, axis=1)


def _mm_body(a_ref, w_ref, o_ref):
    o_ref[...] = _dot(a_ref[...], w_ref[...]).astype(o_ref.dtype)


def _mm_rope_body(a_ref, w_ref, cos_ref, sin_ref, o_ref, *, rope_slabs):
    y = _dot(a_ref[...], w_ref[...])
    o_ref[...] = _rope_tile(y, cos_ref[...], sin_ref[...], rope_slabs).astype(o_ref.dtype)


def _mm_res_body(a_ref, w_ref, r_ref, o_ref):
    @pl.when(pl.program_id(2) == 0)
    def _():
        o_ref[...] = r_ref[...]

    o_ref[...] += _dot(a_ref[...], w_ref[...])


def _mm_gate_up_body(a_ref, wg_ref, wu_ref, o_ref):
    a = a_ref[...]
    o_ref[...] = (jax.nn.silu(_dot(a, wg_ref[...])) * _dot(a, wu_ref[...])).astype(o_ref.dtype)


def _mm_merge_body(a1_ref, w1_ref, a2_ref, w2_ref, g1_ref, g2_ref, o_ref):
    y1 = _dot(a1_ref[...], w1_ref[...])
    y2 = _dot(a2_ref[...], w2_ref[...])
    o_ref[...] = (jax.nn.sigmoid(g1_ref[...]) * y1 + jax.nn.sigmoid(g2_ref[...]) * y2).astype(o_ref.dtype)


def _tile(n, pref):
    return pref if n % pref == 0 else n


def _matmul(a, w, out_dtype, *, tm, tn, name, cols=None):
    m, k = a.shape
    c0, c1 = cols or (0, w.shape[1])
    n = c1 - c0
    tm, tn = _tile(m, tm), _tile(n, tn)
    j0 = c0 // tn
    assert j0 * tn == c0
    return pl.pallas_call(
        _mm_body,
        grid=(m // tm, n // tn),
        in_specs=[pl.BlockSpec((tm, k), lambda i, j: (i, 0)), pl.BlockSpec((k, tn), lambda i, j: (0, j0 + j))],
        out_specs=pl.BlockSpec((tm, tn), lambda i, j: (i, j)),
        out_shape=jax.ShapeDtypeStruct((m, n), out_dtype),
        compiler_params=_params("arbitrary", "arbitrary"),
        name=name,
    )(a, w)


def _matmul_rope(a, w, cos, sin, out_dtype, *, tm, tn, rope_slabs, name, cols):
    m, k = a.shape
    c0, c1 = cols
    n = c1 - c0
    tm, tn = _tile(m, tm), _tile(n, tn)
    j0 = c0 // tn
    assert j0 * tn == c0
    pos_tiles = cos.shape[0] // tm
    return pl.pallas_call(
        functools.partial(_mm_rope_body, rope_slabs=rope_slabs),
        grid=(m // tm, n // tn),
        in_specs=[pl.BlockSpec((tm, k), lambda i, j: (i, 0)), pl.BlockSpec((k, tn), lambda i, j: (0, j0 + j)),
                  pl.BlockSpec((tm, LANES), lambda i, j: (i % pos_tiles, 0)),
                  pl.BlockSpec((tm, LANES), lambda i, j: (i % pos_tiles, 0))],
        out_specs=pl.BlockSpec((tm, tn), lambda i, j: (i, j)),
        out_shape=jax.ShapeDtypeStruct((m, n), out_dtype),
        compiler_params=_params("arbitrary", "arbitrary"),
        name=name,
    )(a, w, cos, sin)


def _matmul_residual(a, w, res, *, tm, tn, tk, name):
    m, k = a.shape
    n = w.shape[1]
    tm, tn, tk = _tile(m, tm), _tile(n, tn), _tile(k, tk)
    return pl.pallas_call(
        _mm_res_body,
        grid=(m // tm, n // tn, k // tk),
        in_specs=[pl.BlockSpec((tm, tk), lambda i, j, kk: (i, kk)), pl.BlockSpec((tk, tn), lambda i, j, kk: (kk, j)),
                  pl.BlockSpec((tm, tn), lambda i, j, kk: (i, j))],
        out_specs=pl.BlockSpec((tm, tn), lambda i, j, kk: (i, j)),
        out_shape=jax.ShapeDtypeStruct((m, n), F32),
        compiler_params=_params("arbitrary", "arbitrary", "arbitrary"),
        name=name,
    )(a, w, res)


def _matmul_gate_up(a, wg, wu, *, tm, tn, name):
    m, k = a.shape
    n = wg.shape[1]
    tm = _tile(m, tm)
    return pl.pallas_call(
        _mm_gate_up_body,
        grid=(m // tm, pl.cdiv(n, tn)),
        in_specs=[pl.BlockSpec((tm, k), lambda i, j: (i, 0)), pl.BlockSpec((k, tn), lambda i, j: (0, j)),
                  pl.BlockSpec((k, tn), lambda i, j: (0, j))],
        out_specs=pl.BlockSpec((tm, tn), lambda i, j: (i, j)),
        out_shape=jax.ShapeDtypeStruct((m, n), BF16),
        compiler_params=_params("arbitrary", "arbitrary"),
        name=name,
    )(a, wg, wu)


def _matmul_merge(a1, w1, a2, w2, gates, *, tm, tn, name):
    m, k = a1.shape
    n = w1.shape[1]
    tm, tn = _tile(m, tm), _tile(n, tn)
    nj = n // tn
    return pl.pallas_call(
        _mm_merge_body,
        grid=(m // tm, nj),
        in_specs=[pl.BlockSpec((tm, k), lambda i, j: (i, 0)), pl.BlockSpec((k, tn), lambda i, j: (0, j)),
                  pl.BlockSpec((tm, k), lambda i, j: (i, 0)), pl.BlockSpec((k, tn), lambda i, j: (0, j)),
                  pl.BlockSpec((tm, tn), lambda i, j: (i, j)), pl.BlockSpec((tm, tn), lambda i, j: (i, j + nj))],
        out_specs=pl.BlockSpec((tm, tn), lambda i, j: (i, j)),
        out_shape=jax.ShapeDtypeStruct((m, n), BF16),
        compiler_params=_params("arbitrary", "arbitrary"),
        name=name,
    )(a1, w1, a2, w2, gates, gates)


def _suffix_matrix(t):
    row = lax.broadcasted_iota(I32, (2 * t, 2 * t), 0) % t
    col = lax.broadcasted_iota(I32, (2 * t, 2 * t), 1)
    return jnp.where((row > col) | (col >= t), 1.0, 0.0).astype(BF16)


def _log_one_minus_beta(z):
    return jnp.minimum(-z, 0.0) - jnp.log(1.0 + jnp.exp(-jnp.abs(z)))


def _sb_block(z, v, u, carry, causal):
    sub = LANES
    n_sub = z.shape[1] // sub
    l1m = _log_one_minus_beta(z)
    if causal is not None:
        l1m = jnp.where(causal, l1m, 0.0)
    hi, lo = _split2(l1m)
    tails = [None] * n_sub
    for j in reversed(range(n_sub)):
        sl = slice(j * sub, (j + 1) * sub)
        sums = _dot(jnp.concatenate([hi[:, sl], lo[:, sl]], axis=1), u)
        tails[j] = sums[:, :sub] + carry
        carry = carry + sums[:, sub:]
    tail = tails[0] if n_sub == 1 else jnp.concatenate(tails, axis=1)
    a = jnp.exp(l1m + z + tail)
    if causal is not None:
        a = jnp.where(causal, a, 0.0)
    return _dot(a.astype(BF16), v), carry


def _sb_prompt_body(q_ref, k_ref, v_ref, o_ref, *, tq, tk, chains, scale):
    seq = q_ref.shape[0]
    rows = tq // chains
    u = _suffix_matrix(LANES)
    col_minus_row = lax.broadcasted_iota(I32, (rows, tk), 1) - lax.broadcasted_iota(I32, (rows, tk), 0)

    def load_kv(kb):
        k0 = pl.multiple_of(kb * tk, tk)
        return k_ref[pl.ds(k0, tk), :].astype(BF16), v_ref[pl.ds(k0, tk), :].astype(BF16)

    def q_tile(qi, _):
        q0 = pl.multiple_of(qi * tq, tq)
        qs = [q_ref[pl.ds(pl.multiple_of(q0 + c * rows, rows), rows), :] for c in range(chains)]
        kb_diag = q0 // tk
        k, v = load_kv(kb_diag)
        state = []
        for c in range(chains):
            causal = col_minus_row < q0 + c * rows - kb_diag * tk
            state += _sb_block(_dot_nt(qs[c], k) * scale, v, u, jnp.zeros((rows, LANES), F32), causal)

        def left(t, st):
            k, v = load_kv(kb_diag - 1 - t)
            new = []
            for c in range(chains):
                pv, carry = _sb_block(_dot_nt(qs[c], k) * scale, v, u, st[2 * c + 1], None)
                new += [st[2 * c] + pv, carry]
            return tuple(new)

        state = lax.fori_loop(0, kb_diag, left, tuple(state))
        for c in range(chains):
            o_ref[pl.ds(pl.multiple_of(q0 + c * rows, rows), rows), :] = state[2 * c].astype(o_ref.dtype)
        return 0

    lax.fori_loop(0, seq // tq, q_tile, 0)


def _sb_prompt(q, kv, *, batch, seq):
    h = N_HEADS_SB
    tq, tk, chains = 4 * LANES, 4 * LANES, 2
    assert tk % tq == 0 and seq % tk == 0
    return pl.pallas_call(
        functools.partial(_sb_prompt_body, tq=tq, tk=tk, chains=chains, scale=HEAD_DIM ** -0.5),
        grid=(batch, h),
        in_specs=[pl.BlockSpec((seq, HEAD_DIM), lambda b, i: (b, i)),
                  pl.BlockSpec((seq, HEAD_DIM), lambda b, i: (b, i)),
                  pl.BlockSpec((seq, HEAD_DIM), lambda b, i: (b, h + i))],
        out_specs=pl.BlockSpec((seq, HEAD_DIM), lambda b, i: (b, i)),
        out_shape=jax.ShapeDtypeStruct(q.shape, BF16),
        compiler_params=_params("arbitrary", "arbitrary"),
        name="sb_prompt",
    )(q, kv, kv)


SB_PAGES_PER_STEP = 8
SUBLANES = 8


def _sb_sample_body(pt_ref, q_ref, cache_ref, o_ref, buf_ref, sem_ref, acc_ref, carry_ref, *, n_pages, scale):
    b, pp = pl.program_id(0), pl.program_id(1)
    steps = pl.num_programs(1)
    t = b * steps + pp
    slot = t % 2
    d = N_HEADS_SB * HEAD_DIM
    page = cache_ref.shape[2]
    halves = N_HEADS_SB // SUBLANES
    per_page = 2 * halves

    def copies(seq, step, to_slot):
        out = []
        for j in range(SB_PAGES_PER_STEP):
            pg = pt_ref[seq, n_pages - 1 - (step * SB_PAGES_PER_STEP + j)]
            for kv in range(2):
                for half in range(halves):
                    out.append(pltpu.make_async_copy(
                        cache_ref.at[0, pg, :, kv, pl.ds(half * SUBLANES, SUBLANES), :],
                        buf_ref.at[to_slot, j * per_page + kv * halves + half], sem_ref.at[to_slot]))
        return out

    @pl.when(t == 0)
    def _():
        for c in copies(0, 0, 0):
            c.start()

    @pl.when(t + 1 < pl.num_programs(0) * steps)
    def _():
        for c in copies((t + 1) // steps, (t + 1) % steps, 1 - slot):
            c.start()

    for c in copies(b, pp, slot):
        c.wait()

    @pl.when(pp == 0)
    def _():
        acc_ref[...] = jnp.zeros_like(acc_ref)
        carry_ref[...] = jnp.zeros_like(carry_ref)

    head = lax.broadcasted_iota(I32, (N_HEADS_SB, d), 0)
    lane_head = lax.broadcasted_iota(I32, (N_HEADS_SB, d), 1) // HEAD_DIM
    own = head == lane_head
    q_bd = jnp.where(own, jnp.broadcast_to(q_ref[0], (N_HEADS_SB, d)), 0.0).astype(BF16)
    u = _suffix_matrix(LANES)

    block_rows = page * SUBLANES
    flat = buf_ref.reshape(2 * SB_PAGES_PER_STEP * per_page * block_rows, HEAD_DIM)

    def heads_to_lanes(first_block):
        base = (slot * SB_PAGES_PER_STEP * per_page + first_block) * block_rows
        return jnp.concatenate([flat[pl.ds(base + i * block_rows + h, page, stride=SUBLANES), :]
                                for i in range(halves) for h in range(SUBLANES)], axis=1).astype(BF16)

    acc, carry = acc_ref[...], carry_ref[...]
    for j in range(SB_PAGES_PER_STEP):
        k = heads_to_lanes(j * per_page)
        v = heads_to_lanes(j * per_page + halves)
        pv, carry = _sb_block(_dot_nt(q_bd, k) * scale, v, u, carry, None)
        acc = acc + pv
    acc_ref[...] = acc
    carry_ref[...] = carry

    @pl.when(pp == pl.num_programs(1) - 1)
    def _():
        o_ref[0] = jnp.sum(jnp.where(own, acc, 0.0), axis=0, keepdims=True)


def _sb_sample(q, cache, page_table):
    s, n_pages = page_table.shape
    d = N_HEADS_SB * HEAD_DIM
    page = cache.shape[2]
    assert page == LANES and n_pages % SB_PAGES_PER_STEP == 0
    steps = n_pages // SB_PAGES_PER_STEP
    n_blocks = SB_PAGES_PER_STEP * 2 * N_HEADS_SB // SUBLANES
    grid_spec = pltpu.PrefetchScalarGridSpec(
        num_scalar_prefetch=1,
        grid=(s, steps),
        in_specs=[pl.BlockSpec((1, 1, d), lambda b, p, pt: (b, 0, 0)), pl.BlockSpec(memory_space=pl.ANY)],
        out_specs=pl.BlockSpec((1, 1, d), lambda b, p, pt: (b, 0, 0)),
        scratch_shapes=[pltpu.VMEM((2, n_blocks, page, SUBLANES, HEAD_DIM), F32), pltpu.SemaphoreType.DMA((2,)),
                        pltpu.VMEM((N_HEADS_SB, d), F32), pltpu.VMEM((N_HEADS_SB, LANES), F32)],
    )
    return pl.pallas_call(
        functools.partial(_sb_sample_body, n_pages=n_pages, scale=HEAD_DIM ** -0.5),
        grid_spec=grid_spec,
        out_shape=jax.ShapeDtypeStruct((s, 1, d), F32),
        compiler_params=_params("arbitrary", "arbitrary"),
        name="sb_sample",
    )(page_table, q, cache)


def _compress_rows(x, pe_ref, w1_ref, w2_ref, kind, n_cmp):
    c = LANES
    pa = _dot((x + pe_ref[kind, 0]).astype(BF16), w1_ref[kind, 0])
    pb = _dot((x + pe_ref[kind, 1]).astype(BF16), w1_ref[kind, 1])
    pb = jnp.concatenate([pltpu.roll(pb[i:i + c], c - 1, 0) for i in range(0, x.shape[0], c)], axis=0)
    out = _dot(jax.nn.gelu(pa + pb).astype(BF16), w2_ref[kind])
    row = lax.broadcasted_iota(I32, out.shape, 0) % c
    return jnp.where(row < n_cmp, out, 0.0)


def _compress_prompt_body(*refs, n_cmp):
    rows, (pe_ref, w1_ref, w2_ref, o_ref) = refs[:2 * N_KV_NSA], refs[2 * N_KV_NSA:]
    chunks = rows[0].shape[0] // CMP_STRIDE
    for kind in range(2):
        xs = [jnp.concatenate([rows[kind * N_KV_NSA + g][pl.ds(l, chunks, stride=CMP_STRIDE), :]
                               for l in range(CMP_STRIDE)], axis=1) for g in range(N_KV_NSA)]
        out = _compress_rows(jnp.concatenate(xs, axis=0), pe_ref, w1_ref, w2_ref, kind, n_cmp)
        for g in range(N_KV_NSA):
            o_ref[0, kind, g] = out[g * LANES:(g + 1) * LANES]


def _compress_prompt(nkv, pe, w1, w2, *, batch, seq):
    chunks = seq // CMP_STRIDE
    assert chunks == LANES
    n_cmp = (seq - CMP_BLOCK) // CMP_STRIDE + 1
    col = lambda c: pl.BlockSpec((seq, HEAD_DIM), lambda b, c=c: (b, c))
    return pl.pallas_call(
        functools.partial(_compress_prompt_body, n_cmp=n_cmp),
        grid=(batch,),
        in_specs=[col(c) for c in range(2 * N_KV_NSA)]
        + [pl.BlockSpec(pe.shape, lambda b: (0, 0, 0, 0)),
           pl.BlockSpec(w1.shape, lambda b: (0, 0, 0, 0)),
           pl.BlockSpec(w2.shape, lambda b: (0, 0, 0))],
        out_specs=pl.BlockSpec((1, 2, N_KV_NSA, chunks, HEAD_DIM), lambda b: (b, 0, 0, 0, 0)),
        out_shape=jax.ShapeDtypeStruct((batch, 2, N_KV_NSA, chunks, HEAD_DIM), F32),
        compiler_params=_params("arbitrary"),
        name="compress_prompt",
    )(*([nkv] * (2 * N_KV_NSA)), pe, w1, w2)


def _overlap(n_idx, m_idx, n_cmp, n_sel):
    d = n_idx * CMP_STRIDE - m_idx * SEL_BLOCK
    d = jnp.where(n_idx < n_cmp, d, SEL_BLOCK)
    d = jnp.where(m_idx < n_sel, d, SEL_BLOCK)
    return jnp.where(d > -CMP_BLOCK, d, SEL_BLOCK) < SEL_BLOCK


def _force_blocks(imp, m_idx, cur):
    forced = (m_idx == 0) | (m_idx == cur) | (m_idx == cur - 1)
    imp = jnp.where(forced, BIG, imp)
    return jnp.where(m_idx <= cur, imp, -BIG)


LOG2_E = 1.4426950408889634


def _attend(qs, k_ref, v_ref, lo, hi, tk, mask_fn, acc_ref, m_ref, scale, first_tile_hits_every_row, chains):
    tq = m_ref.shape[0] // NSA_GROUP
    per_chain = NSA_GROUP // chains
    c2 = scale * LOG2_E
    acc_ref[...] = jnp.zeros_like(acc_ref)
    m_ref[...] = jnp.full(m_ref.shape, NEG_INF, F32)
    ones = jnp.ones((tk, LANES), BF16)

    def step(kt, c):
        k0 = pl.multiple_of(kt * tk, tk)
        kb = k_ref[pl.ds(k0, tk), :].astype(BF16)
        vb = jnp.concatenate([v_ref[pl.ds(k0, tk), :].astype(BF16), ones], axis=1)
        mask = mask_fn(k0)
        for chain in range(chains):
            base = chain * per_chain * tq
            s = _dot_nt(qs[base:base + per_chain * tq], kb)
            ps, alphas = [], []
            for r in range(per_chain):
                sl = slice(base + r * tq, base + (r + 1) * tq)
                s_r = jnp.where(mask, s[r * tq:(r + 1) * tq], NEG_INF)
                m_prev = m_ref[sl, :]
                m_new = jnp.maximum(m_prev, jnp.max(s_r, axis=-1, keepdims=True))
                alphas.append(jnp.exp2((m_prev - m_new) * c2))
                m_ref[sl, :] = m_new
                p = jnp.exp2((s_r - _lanes(m_new, tk)) * c2)
                if not first_tile_hits_every_row:
                    p = jnp.where(mask, p, 0.0)
                ps.append(p.astype(BF16))
            pv = _dot(jnp.concatenate(ps, axis=0), vb)
            for r in range(per_chain):
                sl = slice(base + r * tq, base + (r + 1) * tq)
                acc_ref[sl, :] = acc_ref[sl, :] * _lanes(alphas[r], 2 * LANES) + pv[r * tq:(r + 1) * tq]
        return c

    lax.fori_loop(lo, hi, step, 0)
    acc = acc_ref[...]
    return acc[:, :LANES] / jnp.maximum(acc[:, LANES:], 1e-30)


def _nsa_prompt_body(q_ref, kc_ref, vc_ref, ks_ref, vs_ref, kw_ref, vw_ref, gate_ref, o_ref, acc_ref, m_ref,
                     *, n_cmp, scale):
    tq = q_ref.shape[0]
    seq = ks_ref.shape[0]
    n_sel = seq // SEL_BLOCK
    qi = pl.program_id(2)
    q0 = qi * tq
    q = q_ref[...]
    qs = jnp.concatenate([q[:, r * HEAD_DIM:(r + 1) * HEAD_DIM] for r in range(NSA_GROUP)], axis=0)
    kc = kc_ref[0, 0, 0].astype(BF16)

    n_idx = lax.broadcasted_iota(I32, (LANES, tq), 0)
    qpos_l = q0 + lax.broadcasted_iota(I32, (LANES, tq), 1)
    vis_t = jnp.where(n_idx < n_cmp, n_idx * CMP_STRIDE + CMP_BLOCK - 1, seq) <= qpos_l
    s_t = _dot_nt(kc, qs)
    p_sum = jnp.zeros((LANES, tq), F32)
    for r in range(NSA_GROUP):
        s_r = jnp.where(vis_t, s_t[:, r * tq:(r + 1) * tq] * scale, NEG_INF)
        e = jnp.where(vis_t, jnp.exp(s_r - jnp.max(s_r, axis=0, keepdims=True)), 0.0)
        p_sum = p_sum + e / jnp.maximum(jnp.sum(e, axis=0, keepdims=True), 1e-30)
    m_row = lax.broadcasted_iota(I32, (LANES, LANES), 0)
    n_col = lax.broadcasted_iota(I32, (LANES, LANES), 1)
    ov_t = jnp.where(_overlap(n_col, m_row, n_cmp, n_sel), 1.0, 0.0).astype(BF16)
    imp = sum(_dot(ov_t, part) for part in _split3(p_sum))[:n_sel]
    m_idx = lax.broadcasted_iota(I32, (n_sel, tq), 0)
    cur = (q0 + lax.broadcasted_iota(I32, (n_sel, tq), 1)) // SEL_BLOCK
    imp = _force_blocks(imp, m_idx, cur)
    rank = jnp.zeros((n_sel, tq), F32)
    for mp in range(n_sel):
        other = imp[mp:mp + 1, :]
        tie = jnp.where(m_idx > mp, 1.0, 0.0)
        rank = rank + jnp.where(other > imp, 1.0, jnp.where(other == imp, tie, 0.0))
    sel = jnp.where(rank < SEL_TOP_N, 1.0, 0.0)
    sel_t = jnp.concatenate([sel, jnp.zeros((LANES - n_sel, tq), F32)], axis=0).T.astype(BF16)

    def cmp_mask(k0):
        n = lax.broadcasted_iota(I32, (tq, LANES), 1)
        qpos = q0 + lax.broadcasted_iota(I32, (tq, LANES), 0)
        return jnp.where(n < n_cmp, n * CMP_STRIDE + CMP_BLOCK - 1, seq) <= qpos

    tk_s = 4 * LANES

    def sel_mask(k0):
        blk = (k0 + lax.broadcasted_iota(I32, (LANES, tk_s), 1)) // SEL_BLOCK
        expand = jnp.where(blk == lax.broadcasted_iota(I32, (LANES, tk_s), 0), 1.0, 0.0).astype(BF16)
        chosen = _dot(sel_t, expand)
        kpos = k0 + lax.broadcasted_iota(I32, (tq, tk_s), 1)
        qpos = q0 + lax.broadcasted_iota(I32, (tq, tk_s), 0)
        return jnp.where(kpos <= qpos, chosen, 0.0) > 0.5

    tk_w = 2 * LANES

    def win_mask(k0):
        kpos = k0 + lax.broadcasted_iota(I32, (tq, tk_w), 1)
        qpos = q0 + lax.broadcasted_iota(I32, (tq, tk_w), 0)
        back = qpos - kpos
        return jnp.where(back >= 0, back, WINDOW + 1) <= WINDOW

    o_c = _attend(qs, kc_ref.at[0, 0, 0], vc_ref.at[0, 0, 0], 0, 1, LANES, cmp_mask, acc_ref, m_ref, scale, False, 1)
    o_s = _attend(qs, ks_ref, vs_ref, 0, (q0 + tq + tk_s - 1) // tk_s, tk_s, sel_mask, acc_ref, m_ref, scale, True, 4)
    first_w = jnp.maximum(q0 - WINDOW, 0) // tk_w
    o_w = _attend(qs, kw_ref, vw_ref, first_w, (q0 + tq + tk_w - 1) // tk_w, tk_w, win_mask, acc_ref, m_ref, scale,
                  True, 2)

    gate = jax.nn.sigmoid(gate_ref[...])
    for r in range(NSA_GROUP):
        sl = slice(r * tq, (r + 1) * tq)
        o = (gate[:, 3 * r:3 * r + 1] * o_c[sl] + gate[:, 3 * r + 1:3 * r + 2] * o_s[sl]
             + gate[:, 3 * r + 2:3 * r + 3] * o_w[sl])
        o_ref[:, r * HEAD_DIM:(r + 1) * HEAD_DIM] = o.astype(o_ref.dtype)


def _nsa_prompt(q, cmp, nkv, win, gates, *, batch, seq):
    tq = LANES
    assert tq == LANES and seq % (4 * LANES) == 0
    nq = seq // tq
    gw = NSA_GROUP * HEAD_DIM
    n_cmp = (seq - CMP_BLOCK) // CMP_STRIDE + 1
    rows = lambda c: pl.BlockSpec((seq, HEAD_DIM), lambda b, g, i, c=c: (b, c * N_KV_NSA + g))
    cmp_spec = lambda kind: pl.BlockSpec((1, 1, 1, LANES, HEAD_DIM), lambda b, g, i, kind=kind: (b, kind, g, 0, 0))
    return pl.pallas_call(
        functools.partial(_nsa_prompt_body, n_cmp=n_cmp, scale=HEAD_DIM ** -0.5),
        grid=(batch, N_KV_NSA, nq),
        in_specs=[pl.BlockSpec((tq, gw), lambda b, g, i: (b * nq + i, g)),
                  cmp_spec(0), cmp_spec(1), rows(2), rows(3), rows(0), rows(1),
                  pl.BlockSpec((tq, LANES), lambda b, g, i: (b * nq + i, g))],
        out_specs=pl.BlockSpec((tq, gw), lambda b, g, i: (b * nq + i, g)),
        out_shape=jax.ShapeDtypeStruct(q.shape, BF16),
        scratch_shapes=[pltpu.VMEM((NSA_GROUP * tq, 2 * LANES), F32), pltpu.VMEM((NSA_GROUP * tq, LANES), F32)],
        compiler_params=_params("arbitrary", "arbitrary", "arbitrary"),
        name="nsa_prompt",
    )(q, cmp, cmp, nkv, nkv, win, win, gates)


def _softmax_rows(s, vis):
    s = jnp.where(vis, s, NEG_INF)
    e = jnp.where(vis, jnp.exp(s - jnp.max(s, axis=-1, keepdims=True)), 0.0)
    return e / jnp.maximum(jnp.sum(e, axis=-1, keepdims=True), 1e-30)


def _nsa_sample_body(pt_ref, cache_ref, new_ref, wst_ref, wnew_ref, wrow_ref, q_ref, gate_ref, pe_ref, w1_ref, w2_ref,
                     o_ref, wout_ref, buf_ref, sem_ref, *, n_pages, past, scale):
    b = pl.program_id(0)
    slot = b % 2
    slabs = 4 * N_KV_NSA
    page_chunks = cache_ref.shape[1]
    chunks = n_pages * page_chunks
    pos = past
    n_keys = past + 1
    n_cmp = (n_keys - CMP_BLOCK) // CMP_STRIDE + 1
    n_sel = -(-n_keys // SEL_BLOCK)
    w_buf = wst_ref.shape[1] // (2 * N_KV_NSA)
    gd = N_KV_NSA * HEAD_DIM
    r8 = NSA_GROUP

    def copies(seq, to_slot):
        out = []
        for p in range(n_pages):
            page = pt_ref[seq, p]
            for l in range(CMP_STRIDE):
                out.append(pltpu.make_async_copy(cache_ref.at[page, :, l],
                                                 buf_ref.at[to_slot, l, pl.ds(p * page_chunks, page_chunks)],
                                                 sem_ref.at[to_slot]))
        return out

    @pl.when(b == 0)
    def _():
        for c in copies(0, 0):
            c.start()

    @pl.when(b + 1 < pl.num_programs(0))
    def _():
        for c in copies(b + 1, 1 - slot):
            c.start()

    for c in copies(b, slot):
        c.wait()

    flat = buf_ref.reshape(2 * CMP_STRIDE * chunks * slabs, HEAD_DIM)

    def rows_of(l, slab):
        start = (slot * CMP_STRIDE + l) * (chunks * slabs) + slab
        return flat[pl.ds(start, chunks, stride=slabs), :]

    cmp = []
    for kind in range(2):
        xs = [jnp.concatenate([rows_of(l, kind * N_KV_NSA + g) for l in range(CMP_STRIDE)], axis=1)
              for g in range(N_KV_NSA)]
        cmp.append(_compress_rows(jnp.concatenate(xs, axis=0), pe_ref, w1_ref, w2_ref, kind, n_cmp))

    lane = lax.broadcasted_iota(I32, (r8, LANES), 1)
    sq_row = lax.broadcasted_iota(I32, (LANES, LANES), 0)
    sq_col = lax.broadcasted_iota(I32, (LANES, LANES), 1)
    ov = jnp.where(_overlap(sq_row, sq_col, n_cmp, n_sel), 1.0, 0.0).astype(BF16)
    lower_first = jnp.where(sq_row < sq_col, 1.0, 0.0)
    expand = jnp.where(sq_col // (SEL_BLOCK // CMP_STRIDE) == sq_row, 1.0, 0.0).astype(BF16)
    widx = lax.broadcasted_iota(I32, (r8, w_buf + LANES), 1)
    gate = jax.nn.sigmoid(gate_ref[0])

    def with_new(past_rows, new_row):
        return jnp.concatenate([past_rows, jnp.broadcast_to(new_row, (LANES, HEAD_DIM))], axis=0).astype(BF16)

    for g in range(N_KV_NSA):
        q = q_ref[0, g * r8:(g + 1) * r8, :].astype(BF16)
        kc = cmp[0][g * LANES:(g + 1) * LANES].astype(BF16)
        vc = cmp[1][g * LANES:(g + 1) * LANES].astype(BF16)
        vis_c = jnp.where(lane < n_cmp, lane * CMP_STRIDE + CMP_BLOCK - 1, pos + 1) <= pos
        p_c = _softmax_rows(_dot_nt(q, kc) * scale, vis_c)
        o_c = _dot(p_c.astype(BF16), vc)

        p_sum = jnp.broadcast_to(jnp.sum(p_c, axis=0, keepdims=True), (r8, LANES))
        imp = sum(_dot(part, ov) for part in _split3(p_sum))
        imp = _force_blocks(imp, lane, pos // SEL_BLOCK)
        imp_b = jnp.broadcast_to(imp[0:1], (LANES, LANES))
        imp_c = imp_b.T
        beats = jnp.where(imp_c > imp_b, 1.0, jnp.where(imp_c == imp_b, lower_first, 0.0))
        rank = jnp.sum(beats, axis=0, keepdims=True)
        sel = jnp.broadcast_to(jnp.where(rank < SEL_TOP_N, 1.0, 0.0), (r8, LANES))
        cur = pos // SEL_BLOCK
        new_tile = jnp.where(lane == 0, sel[:, cur:cur + 1], 0.0)
        vis_s = jnp.concatenate([_dot(sel.astype(BF16), expand)] * CMP_STRIDE + [new_tile], axis=1) > 0.5

        c_k = (2 * N_KV_NSA + g) * HEAD_DIM
        c_v = (3 * N_KV_NSA + g) * HEAD_DIM
        past_rows = lambda slab: jnp.concatenate([rows_of(l, slab) for l in range(CMP_STRIDE)], axis=0)
        k_s = with_new(past_rows(2 * N_KV_NSA + g), new_ref[0, :, c_k:c_k + HEAD_DIM])
        v_s = with_new(past_rows(3 * N_KV_NSA + g), new_ref[0, :, c_v:c_v + HEAD_DIM])
        p_s = _softmax_rows(_dot_nt(q, k_s) * scale, vis_s)
        o_s = _dot(p_s.astype(BF16), v_s)

        win_rows = lambda kv: wst_ref[0, pl.ds(kv * N_KV_NSA + g, w_buf, stride=2 * N_KV_NSA), :]
        k_w = with_new(win_rows(0), wnew_ref[0, :, g * HEAD_DIM:(g + 1) * HEAD_DIM])
        v_w = with_new(win_rows(1), wnew_ref[0, :, gd + g * HEAD_DIM:gd + (g + 1) * HEAD_DIM])
        p_w = _softmax_rows(_dot_nt(q, k_w) * scale, widx <= w_buf)
        o_w = _dot(p_w.astype(BF16), v_w)

        gt = gate[g * r8:(g + 1) * r8]
        o_ref[0, g * r8:(g + 1) * r8, :] = gt[:, 0:1] * o_c + gt[:, 1:2] * o_s + gt[:, 2:3] * o_w

    w_rows, new_rows = wst_ref.shape[1], wrow_ref.shape[1]
    wout_ref[0] = pltpu.roll(wst_ref[0], w_rows - new_rows, 0)
    wout_ref[0, pl.ds(w_rows - new_rows, new_rows), :] = wrow_ref[0]


def _nsa_sample(q, gates, new_nkv, win_state, new_win, cache, page_table, pe, w1, w2):
    s, n_pages = page_table.shape
    page_chunks, slabs = cache.shape[1], cache.shape[3]
    chunks = n_pages * page_chunks
    past = chunks * CMP_STRIDE
    assert chunks == LANES and cache.shape[2] == CMP_STRIDE and win_state.shape[1] <= WINDOW * 2 * N_KV_NSA
    new_win_rows = new_win.reshape(s, 2 * N_KV_NSA, HEAD_DIM)
    per_seq = lambda a: pl.BlockSpec((1,) + a.shape[1:], lambda b, pt: (b, 0, 0))
    whole = lambda a: pl.BlockSpec(a.shape, lambda b, pt, n=a.ndim: (0,) * n)
    grid_spec = pltpu.PrefetchScalarGridSpec(
        num_scalar_prefetch=1,
        grid=(s,),
        in_specs=[pl.BlockSpec(memory_space=pl.ANY),
                  per_seq(new_nkv), per_seq(win_state), per_seq(new_win), per_seq(new_win_rows), per_seq(q),
                  per_seq(gates), whole(pe), whole(w1), whole(w2)],
        out_specs=[pl.BlockSpec((1, N_HEADS_NSA, HEAD_DIM), lambda b, pt: (b, 0, 0)), per_seq(win_state)],
        scratch_shapes=[pltpu.VMEM((2, CMP_STRIDE, chunks, slabs, HEAD_DIM), F32), pltpu.SemaphoreType.DMA((2,))],
    )
    return pl.pallas_call(
        functools.partial(_nsa_sample_body, n_pages=n_pages, past=past, scale=HEAD_DIM ** -0.5),
        grid_spec=grid_spec,
        out_shape=[jax.ShapeDtypeStruct((s, N_HEADS_NSA, HEAD_DIM), F32),
                   jax.ShapeDtypeStruct(win_state.shape, win_state.dtype)],
        compiler_params=_params("arbitrary"),
        name="nsa_sample",
    )(page_table, cache, new_nkv, win_state, new_win, new_win_rows, q, gates, pe, w1, w2)


def _rope_tables(pos):
    half = HEAD_DIM // 2
    inv_freq = ROPE_THETA ** (-jnp.arange(half, dtype=F32) / half)
    ang = pos.astype(F32)[:, None] * inv_freq[None, :]
    cos, sin = jnp.cos(ang), jnp.sin(ang)
    return jnp.concatenate([cos, cos], axis=1), jnp.concatenate([-sin, sin], axis=1)


def _prepare_weights(w_in, cmp_k_pe, cmp_k_w1, cmp_k_w2, cmp_v_pe, cmp_v_w1, cmp_v_w2,
                     w_branch_sb, w_branch_nsa, w_out, w_gate, w_up, w_down):
    d_sb = N_HEADS_SB * HEAD_DIM
    d_nsa = N_HEADS_NSA * HEAD_DIM
    d_kv = N_KV_NSA * HEAD_DIM
    cuts = [0, d_sb, 3 * d_sb, 3 * d_sb + d_nsa, 3 * d_sb + d_nsa + 4 * d_kv, 3 * d_sb + d_nsa + 6 * d_kv]
    n_gate = 3 * N_HEADS_NSA
    w_ng = w_in[:, cuts[5]:cuts[5] + n_gate].reshape(-1, N_KV_NSA, 3 * NSA_GROUP)
    w_ng = jnp.pad(w_ng, ((0, 0), (0, 0), (0, LANES - 3 * NSA_GROUP))).reshape(-1, N_KV_NSA * LANES)
    half = CMP_STRIDE * HEAD_DIM
    halves = lambda w1: w1.reshape(2, half, -1)
    return dict(
        w_in=w_in[:, :cuts[5]].astype(BF16),
        cuts=dict(sb_q=(cuts[0], cuts[1]), sb_kv=(cuts[1], cuts[2]), nsa_q=(cuts[2], cuts[3]),
                  nsa_kv=(cuts[3], cuts[4]), nsa_win=(cuts[4], cuts[5])),
        nsa_gate=w_ng.astype(BF16), merge=w_in[:, cuts[5] + n_gate:].astype(BF16),
        pe=jnp.stack([cmp_k_pe.reshape(2, 1, half), cmp_v_pe.reshape(2, 1, half)]),
        w1=jnp.stack([halves(cmp_k_w1), halves(cmp_v_w1)]).astype(BF16),
        w2=jnp.stack([cmp_k_w2, cmp_v_w2]).astype(BF16),
        branch_sb=w_branch_sb.astype(BF16), branch_nsa=w_branch_nsa.astype(BF16), out=w_out.astype(BF16),
        gate=w_gate.astype(BF16), up=w_up.astype(BF16), down=w_down.astype(BF16),
    )


def _project(x, g_attn, w, cos, sin, tm, q_dtype):
    h = _rmsnorm(x, g_attn, BF16, min(tm, 256))
    mm = functools.partial(_matmul, h, tm=tm)
    seg = lambda s, dt: mm(w["w_in"], dt, tn=512, cols=w["cuts"][s], name="proj_" + s)
    rope = lambda s, dt, slabs: _matmul_rope(h, w["w_in"], cos, sin, dt, tm=tm, tn=4 * LANES, rope_slabs=slabs,
                                             cols=w["cuts"][s], name="proj_" + s)
    k_then_v = (True, True, False, False)
    return dict(
        sb_q=seg("sb_q", q_dtype), sb_kv=seg("sb_kv", F32),
        nsa_q=rope("nsa_q", q_dtype, (True,) * 4), nsa_kv=rope("nsa_kv", F32, k_then_v),
        nsa_win=rope("nsa_win", F32, k_then_v),
        nsa_gate=mm(w["nsa_gate"], F32, tn=256, name="proj_nsa_gate"),
        merge=mm(w["merge"], F32, tn=512, name="proj_merge"),
    )


def _finish(x, o_sb, o_nsa, merge, w, g_ffn, g_final, tm):
    merged = _matmul_merge(o_sb, w["branch_sb"], o_nsa, w["branch_nsa"], merge, tm=tm, tn=512, name="branch_merge")
    x = _matmul_residual(merged, w["out"], x, tm=tm, tn=512, tk=merged.shape[1], name="out_proj")
    h = _rmsnorm(x, g_ffn, BF16, min(tm, 256))
    ff = _matmul_gate_up(h, w["gate"], w["up"], tm=tm, tn=512, name="ffn_gate_up")
    x = _matmul_residual(ff, w["down"], x, tm=tm, tn=512, tk=ff.shape[1] // 2, name="ffn_down")
    return _rmsnorm(x, g_final, F32, min(tm, 256))


def kernel(x_prompt, x_sample, cache_sb_kv, cache_nsa_kv, state_nsa_win, page_table, g_attn, w_in, cmp_k_pe, cmp_k_w1, cmp_k_w2, cmp_v_pe, cmp_v_w1, cmp_v_w2, w_branch_sb, w_branch_nsa, w_out, g_ffn, w_gate, w_up, w_down, g_final):
    depth = w_in.shape[0]
    assert depth == 1
    batch, seq, d_model = x_prompt.shape
    n_dec = x_sample.shape[0]
    assert x_sample.shape[1] == 1
    n_pages = page_table.shape[1]
    page = cache_sb_kv.shape[2]
    past = n_pages * page
    win_len = min(WINDOW, seq)
    layer = 0
    w = _prepare_weights(w_in[layer], cmp_k_pe[layer], cmp_k_w1[layer], cmp_k_w2[layer], cmp_v_pe[layer],
                         cmp_v_w1[layer], cmp_v_w2[layer], w_branch_sb[layer], w_branch_nsa[layer], w_out[layer],
                         w_gate[layer], w_up[layer], w_down[layer])

    xp = x_prompt.reshape(batch * seq, d_model)
    cos_p, sin_p = _rope_tables(jnp.arange(seq, dtype=I32))
    pp = _project(xp, g_attn[layer], w, cos_p, sin_p, 1024, BF16)
    o_sb = _sb_prompt(pp["sb_q"], pp["sb_kv"], batch=batch, seq=seq)
    cmp_p = _compress_prompt(pp["nsa_kv"], w["pe"], w["w1"], w["w2"], batch=batch, seq=seq)
    o_nsa = _nsa_prompt(pp["nsa_q"], cmp_p, pp["nsa_kv"], pp["nsa_win"], pp["nsa_gate"], batch=batch, seq=seq)
    y_prompt = _finish(xp, o_sb, o_nsa, pp["merge"], w, g_ffn[layer], g_final, 1024)

    xs = x_sample.reshape(n_dec, d_model)
    cos_s, sin_s = _rope_tables(jnp.full((n_dec,), past, dtype=I32))
    ps = _project(xs, g_attn[layer], w, cos_s, sin_s, n_dec, F32)
    o_sb_s = _sb_sample(ps["sb_q"].reshape(n_dec, 1, -1), cache_sb_kv, page_table)
    nsa_cache = cache_nsa_kv[layer].reshape(cache_nsa_kv.shape[1], page // CMP_STRIDE, CMP_STRIDE, -1, HEAD_DIM)
    gate_s = ps["nsa_gate"].reshape(n_dec, N_KV_NSA, LANES)[:, :, :3 * NSA_GROUP].reshape(n_dec, N_HEADS_NSA, 3)
    gate_s = jnp.pad(gate_s, ((0, 0), (0, 0), (0, LANES - 3)))
    win_state = state_nsa_win[layer].reshape(n_dec, -1, HEAD_DIM)
    o_nsa_s, new_win_s = _nsa_sample(ps["nsa_q"].reshape(n_dec, N_HEADS_NSA, HEAD_DIM), gate_s,
                                     ps["nsa_kv"].reshape(n_dec, 1, -1), win_state,
                                     ps["nsa_win"].reshape(n_dec, 1, -1), nsa_cache, page_table,
                                     w["pe"], w["w1"], w["w2"])
    y_sample = _finish(xs, o_sb_s.reshape(n_dec, -1).astype(BF16), o_nsa_s.reshape(n_dec, -1).astype(BF16),
                       ps["merge"], w, g_ffn[layer], g_final, n_dec)
    return (
        y_prompt.reshape(batch, seq, d_model),
        y_sample.reshape(n_dec, 1, d_model),
        pp["sb_kv"].reshape(depth, batch, seq, 2, N_HEADS_SB, HEAD_DIM),
        ps["sb_kv"].reshape(depth, n_dec, 1, 2, N_HEADS_SB, HEAD_DIM),
        pp["nsa_kv"].reshape(depth, batch, seq, 4, N_KV_NSA, HEAD_DIM),
        ps["nsa_kv"].reshape(depth, n_dec, 1, 4, N_KV_NSA, HEAD_DIM),
        pp["nsa_win"].reshape(batch, seq, -1)[:, seq - win_len:].reshape(depth, batch, win_len, 2, N_KV_NSA, HEAD_DIM),
        new_win_s.reshape(state_nsa_win.shape),
    )
```

```python
import functools

import jax
import jax.numpy as jnp
from jax import lax
from jax.experimental import pallas as pl
from jax.experimental.pallas import tpu as pltpu

F32 = jnp.float32
BF16 = jnp.bfloat16
I32 = jnp.int32

HEAD_DIM = 128
N_HEADS_SB = 16
N_HEADS_NSA = 16
N_KV_NSA = 2
NSA_GROUP = N_HEADS_NSA // N_KV_NSA
CMP_BLOCK = 32
CMP_STRIDE = 16
SEL_BLOCK = 64
SEL_TOP_N = 16
WINDOW = 512
ROPE_THETA = 10000.0
RMS_EPS = 1e-6
NEG_INF = -1e30
BIG = 1e30
LANES = 128
VMEM_LIMIT_BYTES = 48 * 2**20


def _params(*sem):
    return pltpu.CompilerParams(dimension_semantics=sem, vmem_limit_bytes=VMEM_LIMIT_BYTES)


def _dot(a, b):
    return jnp.dot(a, b, preferred_element_type=F32)


def _dot_nt(a, b):
    return lax.dot_general(a, b, (((1,), (1,)), ((), ())), preferred_element_type=F32)


def _lanes(x, n):
    return x if n == LANES else jnp.concatenate([x] * (n // LANES), axis=1)


def _split2(x):
    hi = x.astype(BF16)
    lo = (x - hi.astype(F32)).astype(BF16)
    return hi, lo


def _split3(x):
    hi = x.astype(BF16)
    r1 = x - hi.astype(F32)
    mid = r1.astype(BF16)
    lo = (r1 - mid.astype(F32)).astype(BF16)
    return hi, mid, lo


def _rmsnorm_body(x_ref, g_ref, o_ref):
    x = x_ref[...]
    inv = lax.rsqrt(jnp.mean(x * x, axis=-1, keepdims=True) + RMS_EPS)
    o_ref[...] = (x * inv * g_ref[...]).astype(o_ref.dtype)


def _rmsnorm(x, g, out_dtype, tm):
    m, d = x.shape
    return pl.pallas_call(
        _rmsnorm_body,
        grid=(m // tm,),
        in_specs=[pl.BlockSpec((tm, d), lambda i: (i, 0)), pl.BlockSpec((1, d), lambda i: (0, 0))],
        out_specs=pl.BlockSpec((tm, d), lambda i: (i, 0)),
        out_shape=jax.ShapeDtypeStruct((m, d), out_dtype),
        compiler_params=_params("arbitrary"),
        name="rmsnorm",
    )(x, g.reshape(1, d))


def _rope_tile(y, cos, sin, rope_slabs):
    out = []
    for c, roped in enumerate(rope_slabs):
        x = y[:, c * LANES:(c + 1) * LANES]
        out.append(x * cos + pltpu.roll(x, HEAD_DIM // 2, 1) * sin if roped else x)
    return out[0] if len(out) == 1 else jnp.concatenate(out, axis=1)


def _mm_body(a_ref, w_ref, o_ref):
    o_ref[...] = _dot(a_ref[...], w_ref[...]).astype(o_ref.dtype)


def _mm_rope_body(a_ref, w_ref, cos_ref, sin_ref, o_ref, *, rope_slabs):
    y = _dot(a_ref[...], w_ref[...])
    o_ref[...] = _rope_tile(y, cos_ref[...], sin_ref[...], rope_slabs).astype(o_ref.dtype)


def _mm_res_body(a_ref, w_ref, r_ref, o_ref):
    @pl.when(pl.program_id(2) == 0)
    def _():
        o_ref[...] = r_ref[...]

    o_ref[...] += _dot(a_ref[...], w_ref[...])


def _mm_gate_up_body(a_ref, wg_ref, wu_ref, o_ref):
    a = a_ref[...]
    o_ref[...] = (jax.nn.silu(_dot(a, wg_ref[...])) * _dot(a, wu_ref[...])).astype(o_ref.dtype)


def _mm_merge_body(a1_ref, w1_ref, a2_ref, w2_ref, g1_ref, g2_ref, o_ref):
    y1 = _dot(a1_ref[...], w1_ref[...])
    y2 = _dot(a2_ref[...], w2_ref[...])
    o_ref[...] = (jax.nn.sigmoid(g1_ref[...]) * y1 + jax.nn.sigmoid(g2_ref[...]) * y2).astype(o_ref.dtype)


def _tile(n, pref):
    return pref if n % pref == 0 else n


def _matmul(a, w, out_dtype, *, tm, tn, name, cols=None):
    m, k = a.shape
    c0, c1 = cols or (0, w.shape[1])
    n = c1 - c0
    tm, tn = _tile(m, tm), _tile(n, tn)
    j0 = c0 // tn
    assert j0 * tn == c0
    return pl.pallas_call(
        _mm_body,
        grid=(m // tm, n // tn),
        in_specs=[pl.BlockSpec((tm, k), lambda i, j: (i, 0)), pl.BlockSpec((k, tn), lambda i, j: (0, j0 + j))],
        out_specs=pl.BlockSpec((tm, tn), lambda i, j: (i, j)),
        out_shape=jax.ShapeDtypeStruct((m, n), out_dtype),
        compiler_params=_params("arbitrary", "arbitrary"),
        name=name,
    )(a, w)


def _matmul_rope(a, w, cos, sin, out_dtype, *, tm, tn, rope_slabs, name, cols):
    m, k = a.shape
    c0, c1 = cols
    n = c1 - c0
    tm, tn = _tile(m, tm), _tile(n, tn)
    j0 = c0 // tn
    assert j0 * tn == c0
    pos_tiles = cos.shape[0] // tm
    return pl.pallas_call(
        functools.partial(_mm_rope_body, rope_slabs=rope_slabs),
        grid=(m // tm, n // tn),
        in_specs=[pl.BlockSpec((tm, k), lambda i, j: (i, 0)), pl.BlockSpec((k, tn), lambda i, j: (0, j0 + j)),
                  pl.BlockSpec((tm, LANES), lambda i, j: (i % pos_tiles, 0)),
                  pl.BlockSpec((tm, LANES), lambda i, j: (i % pos_tiles, 0))],
        out_specs=pl.BlockSpec((tm, tn), lambda i, j: (i, j)),
        out_shape=jax.ShapeDtypeStruct((m, n), out_dtype),
        compiler_params=_params("arbitrary", "arbitrary"),
        name=name,
    )(a, w, cos, sin)


def _matmul_residual(a, w, res, *, tm, tn, tk, name):
    m, k = a.shape
    n = w.shape[1]
    tm, tn, tk = _tile(m, tm), _tile(n, tn), _tile(k, tk)
    return pl.pallas_call(
        _mm_res_body,
        grid=(m // tm, n // tn, k // tk),
        in_specs=[pl.BlockSpec((tm, tk), lambda i, j, kk: (i, kk)), pl.BlockSpec((tk, tn), lambda i, j, kk: (kk, j)),
                  pl.BlockSpec((tm, tn), lambda i, j, kk: (i, j))],
        out_specs=pl.BlockSpec((tm, tn), lambda i, j, kk: (i, j)),
        out_shape=jax.ShapeDtypeStruct((m, n), F32),
        compiler_params=_params("arbitrary", "arbitrary", "arbitrary"),
        name=name,
    )(a, w, res)


def _matmul_gate_up(a, wg, wu, *, tm, tn, name):
    m, k = a.shape
    n = wg.shape[1]
    tm = _tile(m, tm)
    return pl.pallas_call(
        _mm_gate_up_body,
        grid=(m // tm, pl.cdiv(n, tn)),
        in_specs=[pl.BlockSpec((tm, k), lambda i, j: (i, 0)), pl.BlockSpec((k, tn), lambda i, j: (0, j)),
                  pl.BlockSpec((k, tn), lambda i, j: (0, j))],
        out_specs=pl.BlockSpec((tm, tn), lambda i, j: (i, j)),
        out_shape=jax.ShapeDtypeStruct((m, n), BF16),
        compiler_params=_params("arbitrary", "arbitrary"),
        name=name,
    )(a, wg, wu)


def _matmul_merge(a1, w1, a2, w2, gates, *, tm, tn, name):
    m, k = a1.shape
    n = w1.shape[1]
    tm, tn = _tile(m, tm), _tile(n, tn)
    nj = n // tn
    return pl.pallas_call(
        _mm_merge_body,
        grid=(m // tm, nj),
        in_specs=[pl.BlockSpec((tm, k), lambda i, j: (i, 0)), pl.BlockSpec((k, tn), lambda i, j: (0, j)),
                  pl.BlockSpec((tm, k), lambda i, j: (i, 0)), pl.BlockSpec((k, tn), lambda i, j: (0, j)),
                  pl.BlockSpec((tm, tn), lambda i, j: (i, j)), pl.BlockSpec((tm, tn), lambda i, j: (i, j + nj))],
        out_specs=pl.BlockSpec((tm, tn), lambda i, j: (i, j)),
        out_shape=jax.ShapeDtypeStruct((m, n), BF16),
        compiler_params=_params("arbitrary", "arbitrary"),
        name=name,
    )(a1, w1, a2, w2, gates, gates)


def _suffix_matrix(t):
    row = lax.broadcasted_iota(I32, (2 * t, 2 * t), 0) % t
    col = lax.broadcasted_iota(I32, (2 * t, 2 * t), 1)
    return jnp.where((row > col) | (col >= t), 1.0, 0.0).astype(BF16)


def _log_one_minus_beta(z):
    return jnp.minimum(-z, 0.0) - jnp.log(1.0 + jnp.exp(-jnp.abs(z)))


def _sb_block(z, v, u, carry, causal):
    sub = LANES
    n_sub = z.shape[1] // sub
    l1m = _log_one_minus_beta(z)
    if causal is not None:
        l1m = jnp.where(causal, l1m, 0.0)
    hi, lo = _split2(l1m)
    tails = [None] * n_sub
    for j in reversed(range(n_sub)):
        sl = slice(j * sub, (j + 1) * sub)
        sums = _dot(jnp.concatenate([hi[:, sl], lo[:, sl]], axis=1), u)
        tails[j] = sums[:, :sub] + carry
        carry = carry + sums[:, sub:]
    tail = tails[0] if n_sub == 1 else jnp.concatenate(tails, axis=1)
    a = jnp.exp(l1m + z + tail)
    if causal is not None:
        a = jnp.where(causal, a, 0.0)
    return _dot(a.astype(BF16), v), carry


def _sb_prompt_body(q_ref, k_ref, v_ref, o_ref, *, tq, tk, chains, scale):
    seq = q_ref.shape[0]
    rows = tq // chains
    u = _suffix_matrix(LANES)
    col_minus_row = lax.broadcasted_iota(I32, (rows, tk), 1) - lax.broadcasted_iota(I32, (rows, tk), 0)

    def load_kv(kb):
        k0 = pl.multiple_of(kb * tk, tk)
        return k_ref[pl.ds(k0, tk), :].astype(BF16), v_ref[pl.ds(k0, tk), :].astype(BF16)

    def q_tile(qi, _):
        q0 = pl.multiple_of(qi * tq, tq)
        qs = [q_ref[pl.ds(pl.multiple_of(q0 + c * rows, rows), rows), :] for c in range(chains)]
        kb_diag = q0 // tk
        k, v = load_kv(kb_diag)
        state = []
        for c in range(chains):
            seen = (c + 1) * rows
            causal = col_minus_row[:, :seen] < c * rows
            state += _sb_block(_dot_nt(qs[c], k[:seen]) * scale, v[:seen], u, jnp.zeros((rows, LANES), F32), causal)

        def left(t, st):
            k, v = load_kv(kb_diag - 1 - t)
            new = []
            for c in range(chains):
                pv, carry = _sb_block(_dot_nt(qs[c], k) * scale, v, u, st[2 * c + 1], None)
                new += [st[2 * c] + pv, carry]
            return tuple(new)

        state = lax.fori_loop(0, kb_diag, left, tuple(state))
        for c in range(chains):
            o_ref[pl.ds(pl.multiple_of(q0 + c * rows, rows), rows), :] = state[2 * c].astype(o_ref.dtype)
        return 0

    lax.fori_loop(0, seq // tq, q_tile, 0)


def _sb_prompt(q, kv, *, batch, seq):
    h = N_HEADS_SB
    tq, tk, chains = 4 * LANES, 4 * LANES, 2
    assert tk == tq and seq % tk == 0
    return pl.pallas_call(
        functools.partial(_sb_prompt_body, tq=tq, tk=tk, chains=chains, scale=HEAD_DIM ** -0.5),
        grid=(batch, h),
        in_specs=[pl.BlockSpec((seq, HEAD_DIM), lambda b, i: (b, i)),
                  pl.BlockSpec((seq, HEAD_DIM), lambda b, i: (b, i)),
                  pl.BlockSpec((seq, HEAD_DIM), lambda b, i: (b, h + i))],
        out_specs=pl.BlockSpec((seq, HEAD_DIM), lambda b, i: (b, i)),
        out_shape=jax.ShapeDtypeStruct(q.shape, BF16),
        compiler_params=_params("arbitrary", "arbitrary"),
        name="sb_prompt",
    )(q, kv, kv)


SB_PAGES_PER_STEP = 8
SUBLANES = 8


def _sb_sample_body(pt_ref, q_ref, cache_ref, o_ref, buf_ref, sem_ref, acc_ref, carry_ref, *, n_pages, scale):
    b, pp = pl.program_id(0), pl.program_id(1)
    steps = pl.num_programs(1)
    t = b * steps + pp
    slot = t % 2
    d = N_HEADS_SB * HEAD_DIM
    page = cache_ref.shape[2]
    halves = N_HEADS_SB // SUBLANES
    per_page = 2 * halves

    def copies(seq, step, to_slot):
        out = []
        for j in range(SB_PAGES_PER_STEP):
            pg = pt_ref[seq, n_pages - 1 - (step * SB_PAGES_PER_STEP + j)]
            for kv in range(2):
                for half in range(halves):
                    out.append(pltpu.make_async_copy(
                        cache_ref.at[0, pg, :, kv, pl.ds(half * SUBLANES, SUBLANES), :],
                        buf_ref.at[to_slot, j * per_page + kv * halves + half], sem_ref.at[to_slot]))
        return out

    @pl.when(t == 0)
    def _():
        for c in copies(0, 0, 0):
            c.start()

    @pl.when(t + 1 < pl.num_programs(0) * steps)
    def _():
        for c in copies((t + 1) // steps, (t + 1) % steps, 1 - slot):
            c.start()

    for c in copies(b, pp, slot):
        c.wait()

    @pl.when(pp == 0)
    def _():
        acc_ref[...] = jnp.zeros_like(acc_ref)
        carry_ref[...] = jnp.zeros_like(carry_ref)

    head = lax.broadcasted_iota(I32, (N_HEADS_SB, d), 0)
    lane_head = lax.broadcasted_iota(I32, (N_HEADS_SB, d), 1) // HEAD_DIM
    own = head == lane_head
    q_bd = jnp.where(own, jnp.broadcast_to(q_ref[0], (N_HEADS_SB, d)), 0.0).astype(BF16)
    u = _suffix_matrix(LANES)

    block_rows = page * SUBLANES
    flat = buf_ref.reshape(2 * SB_PAGES_PER_STEP * per_page * block_rows, HEAD_DIM)

    def heads_to_lanes(first_block):
        base = (slot * SB_PAGES_PER_STEP * per_page + first_block) * block_rows
        return jnp.concatenate([flat[pl.ds(base + i * block_rows + h, page, stride=SUBLANES), :]
                                for i in range(halves) for h in range(SUBLANES)], axis=1).astype(BF16)

    acc, carry = acc_ref[...], carry_ref[...]
    for j in range(SB_PAGES_PER_STEP):
        k = heads_to_lanes(j * per_page)
        v = heads_to_lanes(j * per_page + halves)
        pv, carry = _sb_block(_dot_nt(q_bd, k) * scale, v, u, carry, None)
        acc = acc + pv
    acc_ref[...] = acc
    carry_ref[...] = carry

    @pl.when(pp == pl.num_programs(1) - 1)
    def _():
        o_ref[0] = jnp.sum(jnp.where(own, acc, 0.0), axis=0, keepdims=True)


def _sb_sample(q, cache, page_table):
    s, n_pages = page_table.shape
    d = N_HEADS_SB * HEAD_DIM
    page = cache.shape[2]
    assert page == LANES and n_pages % SB_PAGES_PER_STEP == 0
    steps = n_pages // SB_PAGES_PER_STEP
    n_blocks = SB_PAGES_PER_STEP * 2 * N_HEADS_SB // SUBLANES
    grid_spec = pltpu.PrefetchScalarGridSpec(
        num_scalar_prefetch=1,
        grid=(s, steps),
        in_specs=[pl.BlockSpec((1, 1, d), lambda b, p, pt: (b, 0, 0)), pl.BlockSpec(memory_space=pl.ANY)],
        out_specs=pl.BlockSpec((1, 1, d), lambda b, p, pt: (b, 0, 0)),
        scratch_shapes=[pltpu.VMEM((2, n_blocks, page, SUBLANES, HEAD_DIM), F32), pltpu.SemaphoreType.DMA((2,)),
                        pltpu.VMEM((N_HEADS_SB, d), F32), pltpu.VMEM((N_HEADS_SB, LANES), F32)],
    )
    return pl.pallas_call(
        functools.partial(_sb_sample_body, n_pages=n_pages, scale=HEAD_DIM ** -0.5),
        grid_spec=grid_spec,
        out_shape=jax.ShapeDtypeStruct((s, 1, d), F32),
        compiler_params=_params("arbitrary", "arbitrary"),
        name="sb_sample",
    )(page_table, q, cache)


def _compress_rows(x, pe_ref, w1_ref, w2_ref, kind, n_cmp):
    c = LANES
    pa = _dot((x + pe_ref[kind, 0]).astype(BF16), w1_ref[kind, 0])
    pb = _dot((x + pe_ref[kind, 1]).astype(BF16), w1_ref[kind, 1])
    pb = jnp.concatenate([pltpu.roll(pb[i:i + c], c - 1, 0) for i in range(0, x.shape[0], c)], axis=0)
    out = _dot(jax.nn.gelu(pa + pb).astype(BF16), w2_ref[kind])
    row = lax.broadcasted_iota(I32, out.shape, 0) % c
    return jnp.where(row < n_cmp, out, 0.0)


def _compress_prompt_body(*refs, n_cmp):
    rows, (pe_ref, w1_ref, w2_ref, o_ref) = refs[:2 * N_KV_NSA], refs[2 * N_KV_NSA:]
    chunks = rows[0].shape[0] // CMP_STRIDE
    for kind in range(2):
        xs = [jnp.concatenate([rows[kind * N_KV_NSA + g][pl.ds(l, chunks, stride=CMP_STRIDE), :]
                               for l in range(CMP_STRIDE)], axis=1) for g in range(N_KV_NSA)]
        out = _compress_rows(jnp.concatenate(xs, axis=0), pe_ref, w1_ref, w2_ref, kind, n_cmp)
        for g in range(N_KV_NSA):
            o_ref[0, kind, g] = out[g * LANES:(g + 1) * LANES]


def _compress_prompt(nkv, pe, w1, w2, *, batch, seq):
    chunks = seq // CMP_STRIDE
    assert chunks == LANES
    n_cmp = (seq - CMP_BLOCK) // CMP_STRIDE + 1
    col = lambda c: pl.BlockSpec((seq, HEAD_DIM), lambda b, c=c: (b, c))
    return pl.pallas_call(
        functools.partial(_compress_prompt_body, n_cmp=n_cmp),
        grid=(batch,),
        in_specs=[col(c) for c in range(2 * N_KV_NSA)]
        + [pl.BlockSpec(pe.shape, lambda b: (0, 0, 0, 0)),
           pl.BlockSpec(w1.shape, lambda b: (0, 0, 0, 0)),
           pl.BlockSpec(w2.shape, lambda b: (0, 0, 0))],
        out_specs=pl.BlockSpec((1, 2, N_KV_NSA, chunks, HEAD_DIM), lambda b: (b, 0, 0, 0, 0)),
        out_shape=jax.ShapeDtypeStruct((batch, 2, N_KV_NSA, chunks, HEAD_DIM), F32),
        compiler_params=_params("arbitrary"),
        name="compress_prompt",
    )(*([nkv] * (2 * N_KV_NSA)), pe, w1, w2)


def _overlap(n_idx, m_idx, n_cmp, n_sel):
    d = n_idx * CMP_STRIDE - m_idx * SEL_BLOCK
    d = jnp.where(n_idx < n_cmp, d, SEL_BLOCK)
    d = jnp.where(m_idx < n_sel, d, SEL_BLOCK)
    return jnp.where(d > -CMP_BLOCK, d, SEL_BLOCK) < SEL_BLOCK


def _force_blocks(imp, m_idx, cur):
    forced = (m_idx == 0) | (m_idx == cur) | (m_idx == cur - 1)
    imp = jnp.where(forced, BIG, imp)
    return jnp.where(m_idx <= cur, imp, -BIG)


LOG2_E = 1.4426950408889634


def _attend(qs, k_ref, v_ref, lo, hi, tk, mask_fn, acc_ref, m_ref, scale, first_tile_hits_every_row, chains):
    tq = m_ref.shape[0] // NSA_GROUP
    per_chain = NSA_GROUP // chains
    c2 = scale * LOG2_E
    acc_ref[...] = jnp.zeros_like(acc_ref)
    m_ref[...] = jnp.full(m_ref.shape, NEG_INF, F32)
    ones = jnp.ones((tk, LANES), BF16)

    def step(kt, c):
        k0 = pl.multiple_of(kt * tk, tk)
        kb = k_ref[pl.ds(k0, tk), :].astype(BF16)
        vb = jnp.concatenate([v_ref[pl.ds(k0, tk), :].astype(BF16), ones], axis=1)
        mask = mask_fn(k0)
        for chain in range(chains):
            base = chain * per_chain * tq
            s = _dot_nt(qs[base:base + per_chain * tq], kb)
            ps, alphas = [], []
            for r in range(per_chain):
                sl = slice(base + r * tq, base + (r + 1) * tq)
                s_r = jnp.where(mask, s[r * tq:(r + 1) * tq], NEG_INF)
                m_prev = m_ref[sl, :]
                m_new = jnp.maximum(m_prev, jnp.max(s_r, axis=-1, keepdims=True))
                alphas.append(jnp.exp2((m_prev - m_new) * c2))
                m_ref[sl, :] = m_new
                p = jnp.exp2((s_r - _lanes(m_new, tk)) * c2)
                if not first_tile_hits_every_row:
                    p = jnp.where(mask, p, 0.0)
                ps.append(p.astype(BF16))
            pv = _dot(jnp.concatenate(ps, axis=0), vb)
            for r in range(per_chain):
                sl = slice(base + r * tq, base + (r + 1) * tq)
                acc_ref[sl, :] = acc_ref[sl, :] * _lanes(alphas[r], 2 * LANES) + pv[r * tq:(r + 1) * tq]
        return c

    lax.fori_loop(lo, hi, step, 0)
    acc = acc_ref[...]
    return acc[:, :LANES] / jnp.maximum(acc[:, LANES:], 1e-30)


def _nsa_prompt_body(q_ref, kc_ref, vc_ref, ks_ref, vs_ref, kw_ref, vw_ref, gate_ref, o_ref, acc_ref, m_ref,
                     *, n_cmp, scale):
    tq = q_ref.shape[0]
    seq = ks_ref.shape[0]
    n_sel = seq // SEL_BLOCK
    qi = pl.program_id(2)
    q0 = qi * tq
    q = q_ref[...]
    qs = jnp.concatenate([q[:, r * HEAD_DIM:(r + 1) * HEAD_DIM] for r in range(NSA_GROUP)], axis=0)
    kc = kc_ref[0, 0, 0].astype(BF16)

    n_idx = lax.broadcasted_iota(I32, (LANES, tq), 0)
    qpos_l = q0 + lax.broadcasted_iota(I32, (LANES, tq), 1)
    vis_t = jnp.where(n_idx < n_cmp, n_idx * CMP_STRIDE + CMP_BLOCK - 1, seq) <= qpos_l
    s_t = _dot_nt(kc, qs)
    p_sum = jnp.zeros((LANES, tq), F32)
    for r in range(NSA_GROUP):
        s_r = jnp.where(vis_t, s_t[:, r * tq:(r + 1) * tq] * scale, NEG_INF)
        e = jnp.where(vis_t, jnp.exp(s_r - jnp.max(s_r, axis=0, keepdims=True)), 0.0)
        p_sum = p_sum + e / jnp.maximum(jnp.sum(e, axis=0, keepdims=True), 1e-30)
    m_row = lax.broadcasted_iota(I32, (LANES, LANES), 0)
    n_col = lax.broadcasted_iota(I32, (LANES, LANES), 1)
    ov_t = jnp.where(_overlap(n_col, m_row, n_cmp, n_sel), 1.0, 0.0).astype(BF16)
    imp = sum(_dot(ov_t, part) for part in _split3(p_sum))[:n_sel]
    m_idx = lax.broadcasted_iota(I32, (n_sel, tq), 0)
    cur = (q0 + lax.broadcasted_iota(I32, (n_sel, tq), 1)) // SEL_BLOCK
    imp = _force_blocks(imp, m_idx, cur)
    rank = jnp.zeros((n_sel, tq), F32)
    for mp in range(n_sel):
        other = imp[mp:mp + 1, :]
        tie = jnp.where(m_idx > mp, 1.0, 0.0)
        rank = rank + jnp.where(other > imp, 1.0, jnp.where(other == imp, tie, 0.0))
    sel = jnp.where(rank < SEL_TOP_N, 1.0, 0.0)
    sel_t = jnp.concatenate([sel, jnp.zeros((LANES - n_sel, tq), F32)], axis=0).T.astype(BF16)

    def cmp_mask(k0):
        n = lax.broadcasted_iota(I32, (tq, LANES), 1)
        qpos = q0 + lax.broadcasted_iota(I32, (tq, LANES), 0)
        return jnp.where(n < n_cmp, n * CMP_STRIDE + CMP_BLOCK - 1, seq) <= qpos

    tk_s = 4 * LANES

    def sel_mask(k0):
        blk = (k0 + lax.broadcasted_iota(I32, (LANES, tk_s), 1)) // SEL_BLOCK
        expand = jnp.where(blk == lax.broadcasted_iota(I32, (LANES, tk_s), 0), 1.0, 0.0).astype(BF16)
        chosen = _dot(sel_t, expand)
        kpos = k0 + lax.broadcasted_iota(I32, (tq, tk_s), 1)
        qpos = q0 + lax.broadcasted_iota(I32, (tq, tk_s), 0)
        return jnp.where(kpos <= qpos, chosen, 0.0) > 0.5

    tk_w = 2 * LANES

    def win_mask(k0):
        kpos = k0 + lax.broadcasted_iota(I32, (tq, tk_w), 1)
        qpos = q0 + lax.broadcasted_iota(I32, (tq, tk_w), 0)
        back = qpos - kpos
        return jnp.where(back >= 0, back, WINDOW + 1) <= WINDOW

    o_c = _attend(qs, kc_ref.at[0, 0, 0], vc_ref.at[0, 0, 0], 0, 1, LANES, cmp_mask, acc_ref, m_ref, scale, False, 1)
    o_s = _attend(qs, ks_ref, vs_ref, 0, (q0 + tq + tk_s - 1) // tk_s, tk_s, sel_mask, acc_ref, m_ref, scale, True, 4)
    first_w = jnp.maximum(q0 - WINDOW, 0) // tk_w
    o_w = _attend(qs, kw_ref, vw_ref, first_w, (q0 + tq + tk_w - 1) // tk_w, tk_w, win_mask, acc_ref, m_ref, scale,
                  True, 2)

    gate = jax.nn.sigmoid(gate_ref[...])
    for r in range(NSA_GROUP):
        sl = slice(r * tq, (r + 1) * tq)
        o = (gate[:, 3 * r:3 * r + 1] * o_c[sl] + gate[:, 3 * r + 1:3 * r + 2] * o_s[sl]
             + gate[:, 3 * r + 2:3 * r + 3] * o_w[sl])
        o_ref[:, r * HEAD_DIM:(r + 1) * HEAD_DIM] = o.astype(o_ref.dtype)


def _nsa_prompt(q, cmp, nkv, win, gates, *, batch, seq):
    tq = LANES
    assert tq == LANES and seq % (4 * LANES) == 0
    nq = seq // tq
    gw = NSA_GROUP * HEAD_DIM
    n_cmp = (seq - CMP_BLOCK) // CMP_STRIDE + 1
    rows = lambda c: pl.BlockSpec((seq, HEAD_DIM), lambda b, g, i, c=c: (b, c * N_KV_NSA + g))
    cmp_spec = lambda kind: pl.BlockSpec((1, 1, 1, LANES, HEAD_DIM), lambda b, g, i, kind=kind: (b, kind, g, 0, 0))
    return pl.pallas_call(
        functools.partial(_nsa_prompt_body, n_cmp=n_cmp, scale=HEAD_DIM ** -0.5),
        grid=(batch, N_KV_NSA, nq),
        in_specs=[pl.BlockSpec((tq, gw), lambda b, g, i: (b * nq + i, g)),
                  cmp_spec(0), cmp_spec(1), rows(2), rows(3), rows(0), rows(1),
                  pl.BlockSpec((tq, LANES), lambda b, g, i: (b * nq + i, g))],
        out_specs=pl.BlockSpec((tq, gw), lambda b, g, i: (b * nq + i, g)),
        out_shape=jax.ShapeDtypeStruct(q.shape, BF16),
        scratch_shapes=[pltpu.VMEM((NSA_GROUP * tq, 2 * LANES), F32), pltpu.VMEM((NSA_GROUP * tq, LANES), F32)],
        compiler_params=_params("arbitrary", "arbitrary", "arbitrary"),
        name="nsa_prompt",
    )(q, cmp, cmp, nkv, nkv, win, win, gates)


def _softmax_rows(s, vis):
    s = jnp.where(vis, s, NEG_INF)
    e = jnp.where(vis, jnp.exp(s - jnp.max(s, axis=-1, keepdims=True)), 0.0)
    return e / jnp.maximum(jnp.sum(e, axis=-1, keepdims=True), 1e-30)


def _nsa_sample_body(pt_ref, cache_ref, new_ref, wst_ref, wnew_ref, wrow_ref, q_ref, gate_ref, pe_ref, w1_ref, w2_ref,
                     o_ref, wout_ref, buf_ref, sem_ref, *, n_pages, past, scale):
    b = pl.program_id(0)
    slot = b % 2
    slabs = 4 * N_KV_NSA
    page_chunks = cache_ref.shape[1]
    chunks = n_pages * page_chunks
    pos = past
    n_keys = past + 1
    n_cmp = (n_keys - CMP_BLOCK) // CMP_STRIDE + 1
    n_sel = -(-n_keys // SEL_BLOCK)
    w_buf = wst_ref.shape[1] // (2 * N_KV_NSA)
    gd = N_KV_NSA * HEAD_DIM
    r8 = NSA_GROUP

    def copies(seq, to_slot):
        out = []
        for p in range(n_pages):
            page = pt_ref[seq, p]
            for l in range(CMP_STRIDE):
                out.append(pltpu.make_async_copy(cache_ref.at[page, :, l],
                                                 buf_ref.at[to_slot, l, pl.ds(p * page_chunks, page_chunks)],
                                                 sem_ref.at[to_slot]))
        return out

    @pl.when(b == 0)
    def _():
        for c in copies(0, 0):
            c.start()

    @pl.when(b + 1 < pl.num_programs(0))
    def _():
        for c in copies(b + 1, 1 - slot):
            c.start()

    for c in copies(b, slot):
        c.wait()

    flat = buf_ref.reshape(2 * CMP_STRIDE * chunks * slabs, HEAD_DIM)

    def rows_of(l, slab):
        start = (slot * CMP_STRIDE + l) * (chunks * slabs) + slab
        return flat[pl.ds(start, chunks, stride=slabs), :]

    cmp = []
    for kind in range(2):
        xs = [jnp.concatenate([rows_of(l, kind * N_KV_NSA + g) for l in range(CMP_STRIDE)], axis=1)
              for g in range(N_KV_NSA)]
        cmp.append(_compress_rows(jnp.concatenate(xs, axis=0), pe_ref, w1_ref, w2_ref, kind, n_cmp))

    lane = lax.broadcasted_iota(I32, (r8, LANES), 1)
    sq_row = lax.broadcasted_iota(I32, (LANES, LANES), 0)
    sq_col = lax.broadcasted_iota(I32, (LANES, LANES), 1)
    ov = jnp.where(_overlap(sq_row, sq_col, n_cmp, n_sel), 1.0, 0.0).astype(BF16)
    lower_first = jnp.where(sq_row < sq_col, 1.0, 0.0)
    expand = jnp.where(sq_col // (SEL_BLOCK // CMP_STRIDE) == sq_row, 1.0, 0.0).astype(BF16)
    widx = lax.broadcasted_iota(I32, (r8, w_buf + LANES), 1)
    gate = jax.nn.sigmoid(gate_ref[0])

    def with_new(past_rows, new_row):
        return jnp.concatenate([past_rows, jnp.broadcast_to(new_row, (LANES, HEAD_DIM))], axis=0).astype(BF16)

    for g in range(N_KV_NSA):
        q = q_ref[0, g * r8:(g + 1) * r8, :].astype(BF16)
        kc = cmp[0][g * LANES:(g + 1) * LANES].astype(BF16)
        vc = cmp[1][g * LANES:(g + 1) * LANES].astype(BF16)
        vis_c = jnp.where(lane < n_cmp, lane * CMP_STRIDE + CMP_BLOCK - 1, pos + 1) <= pos
        p_c = _softmax_rows(_dot_nt(q, kc) * scale, vis_c)
        o_c = _dot(p_c.astype(BF16), vc)

        p_sum = jnp.broadcast_to(jnp.sum(p_c, axis=0, keepdims=True), (r8, LANES))
        imp = sum(_dot(part, ov) for part in _split3(p_sum))
        imp = _force_blocks(imp, lane, pos // SEL_BLOCK)
        imp_b = jnp.broadcast_to(imp[0:1], (LANES, LANES))
        imp_c = imp_b.T
        beats = jnp.where(imp_c > imp_b, 1.0, jnp.where(imp_c == imp_b, lower_first, 0.0))
        rank = jnp.sum(beats, axis=0, keepdims=True)
        sel = jnp.broadcast_to(jnp.where(rank < SEL_TOP_N, 1.0, 0.0), (r8, LANES))
        cur = pos // SEL_BLOCK
        new_tile = jnp.where(lane == 0, sel[:, cur:cur + 1], 0.0)
        vis_s = jnp.concatenate([_dot(sel.astype(BF16), expand)] * CMP_STRIDE + [new_tile], axis=1) > 0.5

        c_k = (2 * N_KV_NSA + g) * HEAD_DIM
        c_v = (3 * N_KV_NSA + g) * HEAD_DIM
        past_rows = lambda slab: jnp.concatenate([rows_of(l, slab) for l in range(CMP_STRIDE)], axis=0)
        k_s = with_new(past_rows(2 * N_KV_NSA + g), new_ref[0, :, c_k:c_k + HEAD_DIM])
        v_s = with_new(past_rows(3 * N_KV_NSA + g), new_ref[0, :, c_v:c_v + HEAD_DIM])
        p_s = _softmax_rows(_dot_nt(q, k_s) * scale, vis_s)
        o_s = _dot(p_s.astype(BF16), v_s)

        win_rows = lambda kv: wst_ref[0, pl.ds(kv * N_KV_NSA + g, w_buf, stride=2 * N_KV_NSA), :]
        k_w = with_new(win_rows(0), wnew_ref[0, :, g * HEAD_DIM:(g + 1) * HEAD_DIM])
        v_w = with_new(win_rows(1), wnew_ref[0, :, gd + g * HEAD_DIM:gd + (g + 1) * HEAD_DIM])
        p_w = _softmax_rows(_dot_nt(q, k_w) * scale, widx <= w_buf)
        o_w = _dot(p_w.astype(BF16), v_w)

        gt = gate[g * r8:(g + 1) * r8]
        o_ref[0, g * r8:(g + 1) * r8, :] = gt[:, 0:1] * o_c + gt[:, 1:2] * o_s + gt[:, 2:3] * o_w

    w_rows, new_rows = wst_ref.shape[1], wrow_ref.shape[1]
    wout_ref[0] = pltpu.roll(wst_ref[0], w_rows - new_rows, 0)
    wout_ref[0, pl.ds(w_rows - new_rows, new_rows), :] = wrow_ref[0]


def _nsa_sample(q, gates, new_nkv, win_state, new_win, cache, page_table, pe, w1, w2):
    s, n_pages = page_table.shape
    page_chunks, slabs = cache.shape[1], cache.shape[3]
    chunks = n_pages * page_chunks
    past = chunks * CMP_STRIDE
    assert chunks == LANES and cache.shape[2] == CMP_STRIDE and win_state.shape[1] <= WINDOW * 2 * N_KV_NSA
    new_win_rows = new_win.reshape(s, 2 * N_KV_NSA, HEAD_DIM)
    per_seq = lambda a: pl.BlockSpec((1,) + a.shape[1:], lambda b, pt: (b, 0, 0))
    whole = lambda a: pl.BlockSpec(a.shape, lambda b, pt, n=a.ndim: (0,) * n)
    grid_spec = pltpu.PrefetchScalarGridSpec(
        num_scalar_prefetch=1,
        grid=(s,),
        in_specs=[pl.BlockSpec(memory_space=pl.ANY),
                  per_seq(new_nkv), per_seq(win_state), per_seq(new_win), per_seq(new_win_rows), per_seq(q),
                  per_seq(gates), whole(pe), whole(w1), whole(w2)],
        out_specs=[pl.BlockSpec((1, N_HEADS_NSA, HEAD_DIM), lambda b, pt: (b, 0, 0)), per_seq(win_state)],
        scratch_shapes=[pltpu.VMEM((2, CMP_STRIDE, chunks, slabs, HEAD_DIM), F32), pltpu.SemaphoreType.DMA((2,))],
    )
    return pl.pallas_call(
        functools.partial(_nsa_sample_body, n_pages=n_pages, past=past, scale=HEAD_DIM ** -0.5),
        grid_spec=grid_spec,
        out_shape=[jax.ShapeDtypeStruct((s, N_HEADS_NSA, HEAD_DIM), F32),
                   jax.ShapeDtypeStruct(win_state.shape, win_state.dtype)],
        compiler_params=_params("arbitrary"),
        name="nsa_sample",
    )(page_table, cache, new_nkv, win_state, new_win, new_win_rows, q, gates, pe, w1, w2)


def _rope_tables(pos):
    half = HEAD_DIM // 2
    inv_freq = ROPE_THETA ** (-jnp.arange(half, dtype=F32) / half)
    ang = pos.astype(F32)[:, None] * inv_freq[None, :]
    cos, sin = jnp.cos(ang), jnp.sin(ang)
    return jnp.concatenate([cos, cos], axis=1), jnp.concatenate([-sin, sin], axis=1)


def _prepare_weights(w_in, cmp_k_pe, cmp_k_w1, cmp_k_w2, cmp_v_pe, cmp_v_w1, cmp_v_w2,
                     w_branch_sb, w_branch_nsa, w_out, w_gate, w_up, w_down):
    d_sb = N_HEADS_SB * HEAD_DIM
    d_nsa = N_HEADS_NSA * HEAD_DIM
    d_kv = N_KV_NSA * HEAD_DIM
    cuts = [0, d_sb, 3 * d_sb, 3 * d_sb + d_nsa, 3 * d_sb + d_nsa + 4 * d_kv, 3 * d_sb + d_nsa + 6 * d_kv]
    w_in = w_in.astype(BF16)
    n_gate = 3 * N_HEADS_NSA
    w_ng = w_in[:, cuts[5]:cuts[5] + n_gate].reshape(-1, N_KV_NSA, 3 * NSA_GROUP)
    w_ng = jnp.pad(w_ng, ((0, 0), (0, 0), (0, LANES - 3 * NSA_GROUP))).reshape(-1, N_KV_NSA * LANES)
    half = CMP_STRIDE * HEAD_DIM
    halves = lambda w1: w1.reshape(2, half, -1)
    return dict(
        w_in=w_in,
        cuts=dict(sb_q=(cuts[0], cuts[1]), sb_kv=(cuts[1], cuts[2]), nsa_q=(cuts[2], cuts[3]),
                  nsa_kv=(cuts[3], cuts[4]), nsa_win=(cuts[4], cuts[5])),
        nsa_gate=w_ng, merge=w_in[:, cuts[5] + n_gate:],
        pe=jnp.stack([cmp_k_pe.reshape(2, 1, half), cmp_v_pe.reshape(2, 1, half)]),
        w1=jnp.stack([halves(cmp_k_w1), halves(cmp_v_w1)]).astype(BF16),
        w2=jnp.stack([cmp_k_w2, cmp_v_w2]).astype(BF16),
        branch_sb=w_branch_sb.astype(BF16), branch_nsa=w_branch_nsa.astype(BF16), out=w_out.astype(BF16),
        gate=w_gate.astype(BF16), up=w_up.astype(BF16), down=w_down.astype(BF16),
    )


def _project(x, g_attn, w, cos, sin, tm, q_dtype):
    h = _rmsnorm(x, g_attn, BF16, min(tm, 256))
    mm = functools.partial(_matmul, h, tm=tm)
    seg = lambda s, dt: mm(w["w_in"], dt, tn=512, cols=w["cuts"][s], name="proj_" + s)
    rope = lambda s, dt, slabs: _matmul_rope(h, w["w_in"], cos, sin, dt, tm=tm, tn=4 * LANES, rope_slabs=slabs,
                                             cols=w["cuts"][s], name="proj_" + s)
    k_then_v = (True, True, False, False)
    return dict(
        sb_q=seg("sb_q", q_dtype), sb_kv=seg("sb_kv", F32),
        nsa_q=rope("nsa_q", q_dtype, (True,) * 4), nsa_kv=rope("nsa_kv", F32, k_then_v),
        nsa_win=rope("nsa_win", F32, k_then_v),
        nsa_gate=mm(w["nsa_gate"], F32, tn=256, name="proj_nsa_gate"),
        merge=mm(w["merge"], F32, tn=512, name="proj_merge"),
    )


def _finish(x, o_sb, o_nsa, merge, w, g_ffn, g_final, tm):
    merged = _matmul_merge(o_sb, w["branch_sb"], o_nsa, w["branch_nsa"], merge, tm=tm, tn=512, name="branch_merge")
    x = _matmul_residual(merged, w["out"], x, tm=tm, tn=512, tk=merged.shape[1], name="out_proj")
    h = _rmsnorm(x, g_ffn, BF16, min(tm, 256))
    ff = _matmul_gate_up(h, w["gate"], w["up"], tm=tm, tn=512, name="ffn_gate_up")
    x = _matmul_residual(ff, w["down"], x, tm=tm, tn=512, tk=ff.shape[1] // 2, name="ffn_down")
    return _rmsnorm(x, g_final, F32, min(tm, 256))


def kernel(x_prompt, x_sample, cache_sb_kv, cache_nsa_kv, state_nsa_win, page_table, g_attn, w_in, cmp_k_pe, cmp_k_w1, cmp_k_w2, cmp_v_pe, cmp_v_w1, cmp_v_w2, w_branch_sb, w_branch_nsa, w_out, g_ffn, w_gate, w_up, w_down, g_final):
    depth = w_in.shape[0]
    assert depth == 1
    batch, seq, d_model = x_prompt.shape
    n_dec = x_sample.shape[0]
    assert x_sample.shape[1] == 1
    n_pages = page_table.shape[1]
    page = cache_sb_kv.shape[2]
    past = n_pages * page
    win_len = min(WINDOW, seq)
    layer = 0
    w = _prepare_weights(w_in[layer], cmp_k_pe[layer], cmp_k_w1[layer], cmp_k_w2[layer], cmp_v_pe[layer],
                         cmp_v_w1[layer], cmp_v_w2[layer], w_branch_sb[layer], w_branch_nsa[layer], w_out[layer],
                         w_gate[layer], w_up[layer], w_down[layer])

    xp = x_prompt.reshape(batch * seq, d_model)
    cos_p, sin_p = _rope_tables(jnp.arange(seq, dtype=I32))
    pp = _project(xp, g_attn[layer], w, cos_p, sin_p, 1024, BF16)
    o_sb = _sb_prompt(pp["sb_q"], pp["sb_kv"], batch=batch, seq=seq)
    cmp_p = _compress_prompt(pp["nsa_kv"], w["pe"], w["w1"], w["w2"], batch=batch, seq=seq)
    o_nsa = _nsa_prompt(pp["nsa_q"], cmp_p, pp["nsa_kv"], pp["nsa_win"], pp["nsa_gate"], batch=batch, seq=seq)
    y_prompt = _finish(xp, o_sb, o_nsa, pp["merge"], w, g_ffn[layer], g_final, 1024)

    xs = x_sample.reshape(n_dec, d_model)
    cos_s, sin_s = _rope_tables(jnp.full((n_dec,), past, dtype=I32))
    ps = _project(xs, g_attn[layer], w, cos_s, sin_s, n_dec, F32)
    o_sb_s = _sb_sample(ps["sb_q"].reshape(n_dec, 1, -1), cache_sb_kv, page_table)
    nsa_cache = cache_nsa_kv[layer].reshape(cache_nsa_kv.shape[1], page // CMP_STRIDE, CMP_STRIDE, -1, HEAD_DIM)
    gate_s = ps["nsa_gate"].reshape(n_dec, N_KV_NSA, LANES)[:, :, :3 * NSA_GROUP].reshape(n_dec, N_HEADS_NSA, 3)
    gate_s = jnp.pad(gate_s, ((0, 0), (0, 0), (0, LANES - 3)))
    win_state = state_nsa_win[layer].reshape(n_dec, -1, HEAD_DIM)
    o_nsa_s, new_win_s = _nsa_sample(ps["nsa_q"].reshape(n_dec, N_HEADS_NSA, HEAD_DIM), gate_s,
                                     ps["nsa_kv"].reshape(n_dec, 1, -1), win_state,
                                     ps["nsa_win"].reshape(n_dec, 1, -1), nsa_cache, page_table,
                                     w["pe"], w["w1"], w["w2"])
    y_sample = _finish(xs, o_sb_s.reshape(n_dec, -1).astype(BF16), o_nsa_s.reshape(n_dec, -1).astype(BF16),
                       ps["merge"], w, g_ffn[layer], g_final, n_dec)
    return (
        y_prompt.reshape(batch, seq, d_model),
        y_sample.reshape(n_dec, 1, d_model),
        pp["sb_kv"].reshape(depth, batch, seq, 2, N_HEADS_SB, HEAD_DIM),
        ps["sb_kv"].reshape(depth, n_dec, 1, 2, N_HEADS_SB, HEAD_DIM),
        pp["nsa_kv"].reshape(depth, batch, seq, 4, N_KV_NSA, HEAD_DIM),
        ps["nsa_kv"].reshape(depth, n_dec, 1, 4, N_KV_NSA, HEAD_DIM),
        pp["nsa_win"].reshape(batch, seq, -1)[:, seq - win_len:].reshape(depth, batch, win_len, 2, N_KV_NSA, HEAD_DIM),
        new_win_s.reshape(state_nsa_win.shape),
    )
```

```python
import functools

import jax
import jax.numpy as jnp
from jax import lax
from jax.experimental import pallas as pl
from jax.experimental.pallas import tpu as pltpu

F32 = jnp.float32
BF16 = jnp.bfloat16
I32 = jnp.int32

HEAD_DIM = 128
N_HEADS_SB = 16
N_HEADS_NSA = 16
N_KV_NSA = 2
NSA_GROUP = N_HEADS_NSA // N_KV_NSA
CMP_BLOCK = 32
CMP_STRIDE = 16
SEL_BLOCK = 64
SEL_TOP_N = 16
WINDOW = 512
ROPE_THETA = 10000.0
RMS_EPS = 1e-6
NEG_INF = -1e30
BIG = 1e30
LANES = 128
VMEM_LIMIT_BYTES = 48 * 2**20


def _params(*sem):
    return pltpu.CompilerParams(dimension_semantics=sem, vmem_limit_bytes=VMEM_LIMIT_BYTES)


def _dot(a, b):
    return jnp.dot(a, b, preferred_element_type=F32)


def _dot_nt(a, b):
    return lax.dot_general(a, b, (((1,), (1,)), ((), ())), preferred_element_type=F32)


def _lanes(x, n):
    return x if n == LANES else jnp.concatenate([x] * (n // LANES), axis=1)


def _split2(x):
    hi = x.astype(BF16)
    lo = (x - hi.astype(F32)).astype(BF16)
    return hi, lo


def _split3(x):
    hi = x.astype(BF16)
    r1 = x - hi.astype(F32)
    mid = r1.astype(BF16)
    lo = (r1 - mid.astype(F32)).astype(BF16)
    return hi, mid, lo


def _rmsnorm_body(x_ref, g_ref, o_ref):
    x = x_ref[...]
    inv = lax.rsqrt(jnp.mean(x * x, axis=-1, keepdims=True) + RMS_EPS)
    o_ref[...] = (x * inv * g_ref[...]).astype(o_ref.dtype)


def _rmsnorm(x, g, out_dtype, tm):
    m, d = x.shape
    return pl.pallas_call(
        _rmsnorm_body,
        grid=(m // tm,),
        in_specs=[pl.BlockSpec((tm, d), lambda i: (i, 0)), pl.BlockSpec((1, d), lambda i: (0, 0))],
        out_specs=pl.BlockSpec((tm, d), lambda i: (i, 0)),
        out_shape=jax.ShapeDtypeStruct((m, d), out_dtype),
        compiler_params=_params("arbitrary"),
        name="rmsnorm",
    )(x, g.reshape(1, d))


def _rope_tile(y, cos, sin, rope_slabs):
    out = []
    for c, roped in enumerate(rope_slabs):
        x = y[:, c * LANES:(c + 1) * LANES]
        out.append(x * cos + pltpu.roll(x, HEAD_DIM // 2, 1) * sin if roped else x)
    return out[0] if len(out) == 1 else jnp.concatenate(out, axis=1)


def _mm_body(a_ref, w_ref, o_ref):
    o_ref[...] = _dot(a_ref[...], w_ref[...]).astype(o_ref.dtype)


def _mm_rope_body(a_ref, w_ref, cos_ref, sin_ref, o_ref, *, rope_slabs):
    y = _dot(a_ref[...], w_ref[...])
    o_ref[...] = _rope_tile(y, cos_ref[...], sin_ref[...], rope_slabs).astype(o_ref.dtype)


def _mm_res_body(a_ref, w_ref, r_ref, o_ref):
    @pl.when(pl.program_id(2) == 0)
    def _():
        o_ref[...] = r_ref[...]

    o_ref[...] += _dot(a_ref[...], w_ref[...])


def _mm_gate_up_body(a_ref, wg_ref, wu_ref, o_ref):
    a = a_ref[...]
    o_ref[...] = (jax.nn.silu(_dot(a, wg_ref[...])) * _dot(a, wu_ref[...])).astype(o_ref.dtype)


def _mm_merge_body(a1_ref, w1_ref, a2_ref, w2_ref, g1_ref, g2_ref, o_ref):
    y1 = _dot(a1_ref[...], w1_ref[...])
    y2 = _dot(a2_ref[...], w2_ref[...])
    o_ref[...] = (jax.nn.sigmoid(g1_ref[...]) * y1 + jax.nn.sigmoid(g2_ref[...]) * y2).astype(o_ref.dtype)


def _tile(n, pref):
    return pref if n % pref == 0 else n


def _matmul(a, w, out_dtype, *, tm, tn, name, cols=None):
    m, k = a.shape
    c0, c1 = cols or (0, w.shape[1])
    n = c1 - c0
    tm, tn = _tile(m, tm), _tile(n, tn)
    j0 = c0 // tn
    assert j0 * tn == c0
    return pl.pallas_call(
        _mm_body,
        grid=(m // tm, n // tn),
        in_specs=[pl.BlockSpec((tm, k), lambda i, j: (i, 0)), pl.BlockSpec((k, tn), lambda i, j: (0, j0 + j))],
        out_specs=pl.BlockSpec((tm, tn), lambda i, j: (i, j)),
        out_shape=jax.ShapeDtypeStruct((m, n), out_dtype),
        compiler_params=_params("arbitrary", "arbitrary"),
        name=name,
    )(a, w)


def _matmul_rope(a, w, cos, sin, out_dtype, *, tm, tn, rope_slabs, name, cols):
    m, k = a.shape
    c0, c1 = cols
    n = c1 - c0
    tm, tn = _tile(m, tm), _tile(n, tn)
    j0 = c0 // tn
    assert j0 * tn == c0
    pos_tiles = cos.shape[0] // tm
    return pl.pallas_call(
        functools.partial(_mm_rope_body, rope_slabs=rope_slabs),
        grid=(m // tm, n // tn),
        in_specs=[pl.BlockSpec((tm, k), lambda i, j: (i, 0)), pl.BlockSpec((k, tn), lambda i, j: (0, j0 + j)),
                  pl.BlockSpec((tm, LANES), lambda i, j: (i % pos_tiles, 0)),
                  pl.BlockSpec((tm, LANES), lambda i, j: (i % pos_tiles, 0))],
        out_specs=pl.BlockSpec((tm, tn), lambda i, j: (i, j)),
        out_shape=jax.ShapeDtypeStruct((m, n), out_dtype),
        compiler_params=_params("arbitrary", "arbitrary"),
        name=name,
    )(a, w, cos, sin)


def _matmul_residual(a, w, res, *, tm, tn, tk, name):
    m, k = a.shape
    n = w.shape[1]
    tm, tn, tk = _tile(m, tm), _tile(n, tn), _tile(k, tk)
    return pl.pallas_call(
        _mm_res_body,
        grid=(m // tm, n // tn, k // tk),
        in_specs=[pl.BlockSpec((tm, tk), lambda i, j, kk: (i, kk)), pl.BlockSpec((tk, tn), lambda i, j, kk: (kk, j)),
                  pl.BlockSpec((tm, tn), lambda i, j, kk: (i, j))],
        out_specs=pl.BlockSpec((tm, tn), lambda i, j, kk: (i, j)),
        out_shape=jax.ShapeDtypeStruct((m, n), F32),
        compiler_params=_params("arbitrary", "arbitrary", "arbitrary"),
        name=name,
    )(a, w, res)


def _matmul_gate_up(a, wg, wu, *, tm, tn, name):
    m, k = a.shape
    n = wg.shape[1]
    tm = _tile(m, tm)
    return pl.pallas_call(
        _mm_gate_up_body,
        grid=(m // tm, pl.cdiv(n, tn)),
        in_specs=[pl.BlockSpec((tm, k), lambda i, j: (i, 0)), pl.BlockSpec((k, tn), lambda i, j: (0, j)),
                  pl.BlockSpec((k, tn), lambda i, j: (0, j))],
        out_specs=pl.BlockSpec((tm, tn), lambda i, j: (i, j)),
        out_shape=jax.ShapeDtypeStruct((m, n), BF16),
        compiler_params=_params("arbitrary", "arbitrary"),
        name=name,
    )(a, wg, wu)


def _matmul_merge(a1, w1, a2, w2, gates, *, tm, tn, name):
    m, k = a1.shape
    n = w1.shape[1]
    tm, tn = _tile(m, tm), _tile(n, tn)
    nj = n // tn
    return pl.pallas_call(
        _mm_merge_body,
        grid=(m // tm, nj),
        in_specs=[pl.BlockSpec((tm, k), lambda i, j: (i, 0)), pl.BlockSpec((k, tn), lambda i, j: (0, j)),
                  pl.BlockSpec((tm, k), lambda i, j: (i, 0)), pl.BlockSpec((k, tn), lambda i, j: (0, j)),
                  pl.BlockSpec((tm, tn), lambda i, j: (i, j)), pl.BlockSpec((tm, tn), lambda i, j: (i, j + nj))],
        out_specs=pl.BlockSpec((tm, tn), lambda i, j: (i, j)),
        out_shape=jax.ShapeDtypeStruct((m, n), BF16),
        compiler_params=_params("arbitrary", "arbitrary"),
        name=name,
    )(a1, w1, a2, w2, gates, gates)


def _suffix_matrix(t):
    row = lax.broadcasted_iota(I32, (2 * t, 2 * t), 0) % t
    col = lax.broadcasted_iota(I32, (2 * t, 2 * t), 1)
    return jnp.where((row > col) | (col >= t), 1.0, 0.0).astype(BF16)


def _log_one_minus_beta(z):
    nz = -z
    return jnp.minimum(nz, 0.0) - jnp.log(1.0 + jnp.exp(jnp.minimum(z, nz)))


def _sb_block(z, v, u, carry, causal):
    sub = LANES
    n_sub = z.shape[1] // sub
    l1m = _log_one_minus_beta(z)
    if causal is not None:
        l1m = jnp.where(causal, l1m, 0.0)
    hi, lo = _split2(l1m)
    tails = [None] * n_sub
    for j in reversed(range(n_sub)):
        sl = slice(j * sub, (j + 1) * sub)
        sums = _dot(jnp.concatenate([hi[:, sl], lo[:, sl]], axis=1), u)
        tails[j] = sums[:, :sub] + carry
        carry = carry + sums[:, sub:]
    tail = tails[0] if n_sub == 1 else jnp.concatenate(tails, axis=1)
    a = jnp.exp(l1m + z + tail)
    if causal is not None:
        a = jnp.where(causal, a, 0.0)
    return _dot(a.astype(BF16), v), carry


def _sb_prompt_body(q_ref, k_ref, v_ref, o_ref, *, tq, tk, chains, scale):
    seq = q_ref.shape[0]
    rows = tq // chains
    u = _suffix_matrix(LANES)
    col_minus_row = lax.broadcasted_iota(I32, (rows, tk), 1) - lax.broadcasted_iota(I32, (rows, tk), 0)

    def load_kv(kb):
        k0 = pl.multiple_of(kb * tk, tk)
        return k_ref[pl.ds(k0, tk), :].astype(BF16), v_ref[pl.ds(k0, tk), :].astype(BF16)

    def q_tile(qi, _):
        q0 = pl.multiple_of(qi * tq, tq)
        qs = [q_ref[pl.ds(pl.multiple_of(q0 + c * rows, rows), rows), :] for c in range(chains)]
        kb_diag = q0 // tk
        k, v = load_kv(kb_diag)
        state = []
        for c in range(chains):
            seen = (c + 1) * rows
            causal = col_minus_row[:, :seen] < c * rows
            state += _sb_block(_dot_nt(qs[c], k[:seen]) * scale, v[:seen], u, jnp.zeros((rows, LANES), F32), causal)

        def left(t, st):
            k, v = load_kv(kb_diag - 1 - t)
            new = []
            for c in range(chains):
                pv, carry = _sb_block(_dot_nt(qs[c], k) * scale, v, u, st[2 * c + 1], None)
                new += [st[2 * c] + pv, carry]
            return tuple(new)

        state = lax.fori_loop(0, kb_diag, left, tuple(state))
        for c in range(chains):
            o_ref[pl.ds(pl.multiple_of(q0 + c * rows, rows), rows), :] = state[2 * c].astype(o_ref.dtype)
        return 0

    lax.fori_loop(0, seq // tq, q_tile, 0)


def _sb_prompt(q, kv, *, batch, seq):
    h = N_HEADS_SB
    tq, tk, chains = 4 * LANES, 4 * LANES, 2
    assert tk == tq and seq % tk == 0
    return pl.pallas_call(
        functools.partial(_sb_prompt_body, tq=tq, tk=tk, chains=chains, scale=HEAD_DIM ** -0.5),
        grid=(batch, h),
        in_specs=[pl.BlockSpec((seq, HEAD_DIM), lambda b, i: (b, i)),
                  pl.BlockSpec((seq, HEAD_DIM), lambda b, i: (b, i)),
                  pl.BlockSpec((seq, HEAD_DIM), lambda b, i: (b, h + i))],
        out_specs=pl.BlockSpec((seq, HEAD_DIM), lambda b, i: (b, i)),
        out_shape=jax.ShapeDtypeStruct(q.shape, BF16),
        compiler_params=_params("arbitrary", "arbitrary"),
        name="sb_prompt",
    )(q, kv, kv)


SB_PAGES_PER_STEP = 8
SUBLANES = 8


def _sb_sample_body(pt_ref, q_ref, cache_ref, o_ref, buf_ref, sem_ref, acc_ref, carry_ref, *, n_pages, scale):
    b, pp = pl.program_id(0), pl.program_id(1)
    steps = pl.num_programs(1)
    t = b * steps + pp
    slot = t % 2
    d = N_HEADS_SB * HEAD_DIM
    page = cache_ref.shape[2]
    halves = N_HEADS_SB // SUBLANES
    per_page = 2 * halves

    def copies(seq, step, to_slot):
        out = []
        for j in range(SB_PAGES_PER_STEP):
            pg = pt_ref[seq, n_pages - 1 - (step * SB_PAGES_PER_STEP + j)]
            for kv in range(2):
                for half in range(halves):
                    out.append(pltpu.make_async_copy(
                        cache_ref.at[0, pg, :, kv, pl.ds(half * SUBLANES, SUBLANES), :],
                        buf_ref.at[to_slot, j * per_page + kv * halves + half], sem_ref.at[to_slot]))
        return out

    @pl.when(t == 0)
    def _():
        for c in copies(0, 0, 0):
            c.start()

    @pl.when(t + 1 < pl.num_programs(0) * steps)
    def _():
        for c in copies((t + 1) // steps, (t + 1) % steps, 1 - slot):
            c.start()

    for c in copies(b, pp, slot):
        c.wait()

    @pl.when(pp == 0)
    def _():
        acc_ref[...] = jnp.zeros_like(acc_ref)
        carry_ref[...] = jnp.zeros_like(carry_ref)

    head = lax.broadcasted_iota(I32, (N_HEADS_SB, d), 0)
    lane_head = lax.broadcasted_iota(I32, (N_HEADS_SB, d), 1) // HEAD_DIM
    own = head == lane_head
    q_bd = jnp.where(own, jnp.broadcast_to(q_ref[0], (N_HEADS_SB, d)), 0.0).astype(BF16)
    u = _suffix_matrix(LANES)

    block_rows = page * SUBLANES
    flat = buf_ref.reshape(2 * SB_PAGES_PER_STEP * per_page * block_rows, HEAD_DIM)

    def heads_to_lanes(first_block):
        base = (slot * SB_PAGES_PER_STEP * per_page + first_block) * block_rows
        return jnp.concatenate([flat[pl.ds(base + i * block_rows + h, page, stride=SUBLANES), :]
                                for i in range(halves) for h in range(SUBLANES)], axis=1).astype(BF16)

    acc, carry = acc_ref[...], carry_ref[...]
    for j in range(SB_PAGES_PER_STEP):
        k = heads_to_lanes(j * per_page)
        v = heads_to_lanes(j * per_page + halves)
        pv, carry = _sb_block(_dot_nt(q_bd, k) * scale, v, u, carry, None)
        acc = acc + pv
    acc_ref[...] = acc
    carry_ref[...] = carry

    @pl.when(pp == pl.num_programs(1) - 1)
    def _():
        o_ref[0] = jnp.sum(jnp.where(own, acc, 0.0), axis=0, keepdims=True)


def _sb_sample(q, cache, page_table):
    s, n_pages = page_table.shape
    d = N_HEADS_SB * HEAD_DIM
    page = cache.shape[2]
    assert page == LANES and n_pages % SB_PAGES_PER_STEP == 0
    steps = n_pages // SB_PAGES_PER_STEP
    n_blocks = SB_PAGES_PER_STEP * 2 * N_HEADS_SB // SUBLANES
    grid_spec = pltpu.PrefetchScalarGridSpec(
        num_scalar_prefetch=1,
        grid=(s, steps),
        in_specs=[pl.BlockSpec((1, 1, d), lambda b, p, pt: (b, 0, 0)), pl.BlockSpec(memory_space=pl.ANY)],
        out_specs=pl.BlockSpec((1, 1, d), lambda b, p, pt: (b, 0, 0)),
        scratch_shapes=[pltpu.VMEM((2, n_blocks, page, SUBLANES, HEAD_DIM), F32), pltpu.SemaphoreType.DMA((2,)),
                        pltpu.VMEM((N_HEADS_SB, d), F32), pltpu.VMEM((N_HEADS_SB, LANES), F32)],
    )
    return pl.pallas_call(
        functools.partial(_sb_sample_body, n_pages=n_pages, scale=HEAD_DIM ** -0.5),
        grid_spec=grid_spec,
        out_shape=jax.ShapeDtypeStruct((s, 1, d), F32),
        compiler_params=_params("arbitrary", "arbitrary"),
        name="sb_sample",
    )(page_table, q, cache)


def _compress_rows(x, pe_ref, w1_ref, w2_ref, kind, n_cmp):
    c = LANES
    pa = _dot((x + pe_ref[kind, 0]).astype(BF16), w1_ref[kind, 0])
    pb = _dot((x + pe_ref[kind, 1]).astype(BF16), w1_ref[kind, 1])
    pb = jnp.concatenate([pltpu.roll(pb[i:i + c], c - 1, 0) for i in range(0, x.shape[0], c)], axis=0)
    out = _dot(jax.nn.gelu(pa + pb).astype(BF16), w2_ref[kind])
    row = lax.broadcasted_iota(I32, out.shape, 0) % c
    return jnp.where(row < n_cmp, out, 0.0)


def _compress_prompt_body(*refs, n_cmp):
    rows, (pe_ref, w1_ref, w2_ref, o_ref) = refs[:2 * N_KV_NSA], refs[2 * N_KV_NSA:]
    chunks = rows[0].shape[0] // CMP_STRIDE
    for kind in range(2):
        xs = [jnp.concatenate([rows[kind * N_KV_NSA + g][pl.ds(l, chunks, stride=CMP_STRIDE), :]
                               for l in range(CMP_STRIDE)], axis=1) for g in range(N_KV_NSA)]
        out = _compress_rows(jnp.concatenate(xs, axis=0), pe_ref, w1_ref, w2_ref, kind, n_cmp)
        for g in range(N_KV_NSA):
            o_ref[0, kind, g] = out[g * LANES:(g + 1) * LANES]


def _compress_prompt(nkv, pe, w1, w2, *, batch, seq):
    chunks = seq // CMP_STRIDE
    assert chunks == LANES
    n_cmp = (seq - CMP_BLOCK) // CMP_STRIDE + 1
    col = lambda c: pl.BlockSpec((seq, HEAD_DIM), lambda b, c=c: (b, c))
    return pl.pallas_call(
        functools.partial(_compress_prompt_body, n_cmp=n_cmp),
        grid=(batch,),
        in_specs=[col(c) for c in range(2 * N_KV_NSA)]
        + [pl.BlockSpec(pe.shape, lambda b: (0, 0, 0, 0)),
           pl.BlockSpec(w1.shape, lambda b: (0, 0, 0, 0)),
           pl.BlockSpec(w2.shape, lambda b: (0, 0, 0))],
        out_specs=pl.BlockSpec((1, 2, N_KV_NSA, chunks, HEAD_DIM), lambda b: (b, 0, 0, 0, 0)),
        out_shape=jax.ShapeDtypeStruct((batch, 2, N_KV_NSA, chunks, HEAD_DIM), F32),
        compiler_params=_params("arbitrary"),
        name="compress_prompt",
    )(*([nkv] * (2 * N_KV_NSA)), pe, w1, w2)


def _overlap(n_idx, m_idx, n_cmp, n_sel):
    d = n_idx * CMP_STRIDE - m_idx * SEL_BLOCK
    d = jnp.where(n_idx < n_cmp, d, SEL_BLOCK)
    d = jnp.where(m_idx < n_sel, d, SEL_BLOCK)
    return jnp.where(d > -CMP_BLOCK, d, SEL_BLOCK) < SEL_BLOCK


def _force_blocks(imp, m_idx, cur):
    forced = (m_idx == 0) | (m_idx == cur) | (m_idx == cur - 1)
    imp = jnp.where(forced, BIG, imp)
    return jnp.where(m_idx <= cur, imp, -BIG)


LOG2_E = 1.4426950408889634


def _attend(qs, k_ref, v_ref, lo, hi, tk, mask_fn, acc_ref, m_ref, scale, first_tile_hits_every_row, chains):
    tq = m_ref.shape[0] // NSA_GROUP
    per_chain = NSA_GROUP // chains
    c2 = scale * LOG2_E
    acc_ref[...] = jnp.zeros_like(acc_ref)
    m_ref[...] = jnp.full(m_ref.shape, NEG_INF, F32)
    ones = jnp.ones((tk, LANES), BF16)

    def step(kt, c):
        k0 = pl.multiple_of(kt * tk, tk)
        kb = k_ref[pl.ds(k0, tk), :].astype(BF16)
        vb = jnp.concatenate([v_ref[pl.ds(k0, tk), :].astype(BF16), ones], axis=1)
        mask = mask_fn(k0)
        for chain in range(chains):
            base = chain * per_chain * tq
            s = _dot_nt(qs[base:base + per_chain * tq], kb)
            ps, alphas = [], []
            for r in range(per_chain):
                sl = slice(base + r * tq, base + (r + 1) * tq)
                s_r = jnp.where(mask, s[r * tq:(r + 1) * tq], NEG_INF)
                m_prev = m_ref[sl, :]
                m_new = jnp.maximum(m_prev, jnp.max(s_r, axis=-1, keepdims=True))
                alphas.append(jnp.exp2((m_prev - m_new) * c2))
                m_ref[sl, :] = m_new
                p = jnp.exp2((s_r - _lanes(m_new, tk)) * c2)
                if not first_tile_hits_every_row:
                    p = jnp.where(mask, p, 0.0)
                ps.append(p.astype(BF16))
            pv = _dot(jnp.concatenate(ps, axis=0), vb)
            for r in range(per_chain):
                sl = slice(base + r * tq, base + (r + 1) * tq)
                acc_ref[sl, :] = acc_ref[sl, :] * _lanes(alphas[r], 2 * LANES) + pv[r * tq:(r + 1) * tq]
        return c

    lax.fori_loop(lo, hi, step, 0)
    acc = acc_ref[...]
    return acc[:, :LANES] / jnp.maximum(acc[:, LANES:], 1e-30)


def _nsa_prompt_body(q_ref, kc_ref, vc_ref, ks_ref, vs_ref, kw_ref, vw_ref, gate_ref, o_ref, acc_ref, m_ref,
                     *, n_cmp, scale):
    tq = q_ref.shape[0]
    seq = ks_ref.shape[0]
    n_sel = seq // SEL_BLOCK
    qi = pl.program_id(2)
    q0 = qi * tq
    q = q_ref[...]
    qs = jnp.concatenate([q[:, r * HEAD_DIM:(r + 1) * HEAD_DIM] for r in range(NSA_GROUP)], axis=0)
    kc = kc_ref[0, 0, 0].astype(BF16)

    n_lane = lax.broadcasted_iota(I32, (tq, LANES), 1)
    qpos_c = q0 + lax.broadcasted_iota(I32, (tq, LANES), 0)
    vis_c = jnp.where(n_lane < n_cmp, n_lane * CMP_STRIDE + CMP_BLOCK - 1, seq) <= qpos_c
    s_c = _dot_nt(qs, kc)
    p_sum = jnp.zeros((tq, LANES), F32)
    p_c = []
    for r in range(NSA_GROUP):
        s_r = jnp.where(vis_c, s_c[r * tq:(r + 1) * tq], NEG_INF)
        e = jnp.where(vis_c, jnp.exp2((s_r - jnp.max(s_r, axis=-1, keepdims=True)) * (scale * LOG2_E)), 0.0)
        p = e / jnp.maximum(jnp.sum(e, axis=-1, keepdims=True), 1e-30)
        p_sum = p_sum + p
        p_c.append(p.astype(BF16))
    o_c = _dot(jnp.concatenate(p_c, axis=0), vc_ref[0, 0, 0].astype(BF16))
    m_row = lax.broadcasted_iota(I32, (LANES, LANES), 0)
    n_col = lax.broadcasted_iota(I32, (LANES, LANES), 1)
    ov_t = jnp.where(_overlap(n_col, m_row, n_cmp, n_sel), 1.0, 0.0).astype(BF16)
    imp = sum(_dot(ov_t, part) for part in _split3(p_sum.T))[:n_sel]
    m_idx = lax.broadcasted_iota(I32, (n_sel, tq), 0)
    cur = (q0 + lax.broadcasted_iota(I32, (n_sel, tq), 1)) // SEL_BLOCK
    imp = _force_blocks(imp, m_idx, cur)
    rank = jnp.zeros((n_sel, tq), F32)
    for mp in range(n_sel):
        other = imp[mp:mp + 1, :]
        tie = jnp.where(m_idx > mp, 1.0, 0.0)
        rank = rank + jnp.where(other > imp, 1.0, jnp.where(other == imp, tie, 0.0))
    sel = jnp.where(rank < SEL_TOP_N, 1.0, 0.0)
    sel_t = jnp.concatenate([sel, jnp.zeros((LANES - n_sel, tq), F32)], axis=0).T.astype(BF16)

    tk_s = 4 * LANES

    def sel_mask(k0):
        blk = (k0 + lax.broadcasted_iota(I32, (LANES, tk_s), 1)) // SEL_BLOCK
        expand = jnp.where(blk == lax.broadcasted_iota(I32, (LANES, tk_s), 0), 1.0, 0.0).astype(BF16)
        chosen = _dot(sel_t, expand)
        kpos = k0 + lax.broadcasted_iota(I32, (tq, tk_s), 1)
        qpos = q0 + lax.broadcasted_iota(I32, (tq, tk_s), 0)
        return jnp.where(kpos <= qpos, chosen, 0.0) > 0.5

    tk_w = 2 * LANES

    def win_mask(k0):
        kpos = k0 + lax.broadcasted_iota(I32, (tq, tk_w), 1)
        qpos = q0 + lax.broadcasted_iota(I32, (tq, tk_w), 0)
        back = qpos - kpos
        return jnp.where(back >= 0, back, WINDOW + 1) <= WINDOW

    o_s = _attend(qs, ks_ref, vs_ref, 0, (q0 + tq + tk_s - 1) // tk_s, tk_s, sel_mask, acc_ref, m_ref, scale, True, 4)
    first_w = jnp.maximum(q0 - WINDOW, 0) // tk_w
    o_w = _attend(qs, kw_ref, vw_ref, first_w, (q0 + tq + tk_w - 1) // tk_w, tk_w, win_mask, acc_ref, m_ref, scale,
                  True, 2)

    gate = jax.nn.sigmoid(gate_ref[...])
    for r in range(NSA_GROUP):
        sl = slice(r * tq, (r + 1) * tq)
        o = (gate[:, 3 * r:3 * r + 1] * o_c[sl] + gate[:, 3 * r + 1:3 * r + 2] * o_s[sl]
             + gate[:, 3 * r + 2:3 * r + 3] * o_w[sl])
        o_ref[:, r * HEAD_DIM:(r + 1) * HEAD_DIM] = o.astype(o_ref.dtype)


def _nsa_prompt(q, cmp, nkv, win, gates, *, batch, seq):
    tq = 2 * LANES
    assert seq % (4 * LANES) == 0
    nq = seq // tq
    gw = NSA_GROUP * HEAD_DIM
    n_cmp = (seq - CMP_BLOCK) // CMP_STRIDE + 1
    rows = lambda c: pl.BlockSpec((seq, HEAD_DIM), lambda b, g, i, c=c: (b, c * N_KV_NSA + g))
    cmp_spec = lambda kind: pl.BlockSpec((1, 1, 1, LANES, HEAD_DIM), lambda b, g, i, kind=kind: (b, kind, g, 0, 0))
    return pl.pallas_call(
        functools.partial(_nsa_prompt_body, n_cmp=n_cmp, scale=HEAD_DIM ** -0.5),
        grid=(batch, N_KV_NSA, nq),
        in_specs=[pl.BlockSpec((tq, gw), lambda b, g, i: (b * nq + i, g)),
                  cmp_spec(0), cmp_spec(1), rows(2), rows(3), rows(0), rows(1),
                  pl.BlockSpec((tq, LANES), lambda b, g, i: (b * nq + i, g))],
        out_specs=pl.BlockSpec((tq, gw), lambda b, g, i: (b * nq + i, g)),
        out_shape=jax.ShapeDtypeStruct(q.shape, BF16),
        scratch_shapes=[pltpu.VMEM((NSA_GROUP * tq, 2 * LANES), F32), pltpu.VMEM((NSA_GROUP * tq, LANES), F32)],
        compiler_params=_params("arbitrary", "arbitrary", "arbitrary"),
        name="nsa_prompt",
    )(q, cmp, cmp, nkv, nkv, win, win, gates)


def _softmax_rows(s, vis):
    s = jnp.where(vis, s, NEG_INF)
    e = jnp.where(vis, jnp.exp(s - jnp.max(s, axis=-1, keepdims=True)), 0.0)
    return e / jnp.maximum(jnp.sum(e, axis=-1, keepdims=True), 1e-30)


def _nsa_sample_body(pt_ref, cache_ref, new_ref, wst_ref, wnew_ref, wrow_ref, q_ref, gate_ref, pe_ref, w1_ref, w2_ref,
                     o_ref, wout_ref, buf_ref, sem_ref, *, n_pages, past, scale):
    b = pl.program_id(0)
    slot = b % 2
    slabs = 4 * N_KV_NSA
    page_chunks = cache_ref.shape[1]
    chunks = n_pages * page_chunks
    pos = past
    n_keys = past + 1
    n_cmp = (n_keys - CMP_BLOCK) // CMP_STRIDE + 1
    n_sel = -(-n_keys // SEL_BLOCK)
    w_buf = wst_ref.shape[1] // (2 * N_KV_NSA)
    gd = N_KV_NSA * HEAD_DIM
    r8 = NSA_GROUP

    def copies(seq, to_slot):
        out = []
        for p in range(n_pages):
            page = pt_ref[seq, p]
            for l in range(CMP_STRIDE):
                out.append(pltpu.make_async_copy(cache_ref.at[page, :, l],
                                                 buf_ref.at[to_slot, l, pl.ds(p * page_chunks, page_chunks)],
                                                 sem_ref.at[to_slot]))
        return out

    @pl.when(b == 0)
    def _():
        for c in copies(0, 0):
            c.start()

    @pl.when(b + 1 < pl.num_programs(0))
    def _():
        for c in copies(b + 1, 1 - slot):
            c.start()

    for c in copies(b, slot):
        c.wait()

    flat = buf_ref.reshape(2 * CMP_STRIDE * chunks * slabs, HEAD_DIM)

    def rows_of(l, slab):
        start = (slot * CMP_STRIDE + l) * (chunks * slabs) + slab
        return flat[pl.ds(start, chunks, stride=slabs), :]

    cmp = []
    for kind in range(2):
        xs = [jnp.concatenate([rows_of(l, kind * N_KV_NSA + g) for l in range(CMP_STRIDE)], axis=1)
              for g in range(N_KV_NSA)]
        cmp.append(_compress_rows(jnp.concatenate(xs, axis=0), pe_ref, w1_ref, w2_ref, kind, n_cmp))

    lane = lax.broadcasted_iota(I32, (r8, LANES), 1)
    sq_row = lax.broadcasted_iota(I32, (LANES, LANES), 0)
    sq_col = lax.broadcasted_iota(I32, (LANES, LANES), 1)
    ov = jnp.where(_overlap(sq_row, sq_col, n_cmp, n_sel), 1.0, 0.0).astype(BF16)
    lower_first = jnp.where(sq_row < sq_col, 1.0, 0.0)
    expand = jnp.where(sq_col // (SEL_BLOCK // CMP_STRIDE) == sq_row, 1.0, 0.0).astype(BF16)
    widx = lax.broadcasted_iota(I32, (r8, w_buf + LANES), 1)
    gate = jax.nn.sigmoid(gate_ref[0])

    def with_new(past_rows, new_row):
        return jnp.concatenate([past_rows, jnp.broadcast_to(new_row, (LANES, HEAD_DIM))], axis=0).astype(BF16)

    for g in range(N_KV_NSA):
        q = q_ref[0, g * r8:(g + 1) * r8, :].astype(BF16)
        kc = cmp[0][g * LANES:(g + 1) * LANES].astype(BF16)
        vc = cmp[1][g * LANES:(g + 1) * LANES].astype(BF16)
        vis_c = jnp.where(lane < n_cmp, lane * CMP_STRIDE + CMP_BLOCK - 1, pos + 1) <= pos
        p_c = _softmax_rows(_dot_nt(q, kc) * scale, vis_c)
        o_c = _dot(p_c.astype(BF16), vc)

        p_sum = jnp.broadcast_to(jnp.sum(p_c, axis=0, keepdims=True), (r8, LANES))
        imp = sum(_dot(part, ov) for part in _split3(p_sum))
        imp = _force_blocks(imp, lane, pos // SEL_BLOCK)
        imp_b = jnp.broadcast_to(imp[0:1], (LANES, LANES))
        imp_c = imp_b.T
        beats = jnp.where(imp_c > imp_b, 1.0, jnp.where(imp_c == imp_b, lower_first, 0.0))
        rank = jnp.sum(beats, axis=0, keepdims=True)
        sel = jnp.broadcast_to(jnp.where(rank < SEL_TOP_N, 1.0, 0.0), (r8, LANES))
        cur = pos // SEL_BLOCK
        new_tile = jnp.where(lane == 0, sel[:, cur:cur + 1], 0.0)
        vis_s = jnp.concatenate([_dot(sel.astype(BF16), expand)] * CMP_STRIDE + [new_tile], axis=1) > 0.5

        c_k = (2 * N_KV_NSA + g) * HEAD_DIM
        c_v = (3 * N_KV_NSA + g) * HEAD_DIM
        past_rows = lambda slab: jnp.concatenate([rows_of(l, slab) for l in range(CMP_STRIDE)], axis=0)
        k_s = with_new(past_rows(2 * N_KV_NSA + g), new_ref[0, :, c_k:c_k + HEAD_DIM])
        v_s = with_new(past_rows(3 * N_KV_NSA + g), new_ref[0, :, c_v:c_v + HEAD_DIM])
        p_s = _softmax_rows(_dot_nt(q, k_s) * scale, vis_s)
        o_s = _dot(p_s.astype(BF16), v_s)

        win_rows = lambda kv: wst_ref[0, pl.ds(kv * N_KV_NSA + g, w_buf, stride=2 * N_KV_NSA), :]
        k_w = with_new(win_rows(0), wnew_ref[0, :, g * HEAD_DIM:(g + 1) * HEAD_DIM])
        v_w = with_new(win_rows(1), wnew_ref[0, :, gd + g * HEAD_DIM:gd + (g + 1) * HEAD_DIM])
        p_w = _softmax_rows(_dot_nt(q, k_w) * scale, widx <= w_buf)
        o_w = _dot(p_w.astype(BF16), v_w)

        gt = gate[g * r8:(g + 1) * r8]
        o_ref[0, g * r8:(g + 1) * r8, :] = gt[:, 0:1] * o_c + gt[:, 1:2] * o_s + gt[:, 2:3] * o_w

    w_rows, new_rows = wst_ref.shape[1], wrow_ref.shape[1]
    wout_ref[0] = pltpu.roll(wst_ref[0], w_rows - new_rows, 0)
    wout_ref[0, pl.ds(w_rows - new_rows, new_rows), :] = wrow_ref[0]


def _nsa_sample(q, gates, new_nkv, win_state, new_win, cache, page_table, pe, w1, w2):
    s, n_pages = page_table.shape
    page_chunks, slabs = cache.shape[1], cache.shape[3]
    chunks = n_pages * page_chunks
    past = chunks * CMP_STRIDE
    assert chunks == LANES and cache.shape[2] == CMP_STRIDE and win_state.shape[1] <= WINDOW * 2 * N_KV_NSA
    new_win_rows = new_win.reshape(s, 2 * N_KV_NSA, HEAD_DIM)
    per_seq = lambda a: pl.BlockSpec((1,) + a.shape[1:], lambda b, pt: (b, 0, 0))
    whole = lambda a: pl.BlockSpec(a.shape, lambda b, pt, n=a.ndim: (0,) * n)
    grid_spec = pltpu.PrefetchScalarGridSpec(
        num_scalar_prefetch=1,
        grid=(s,),
        in_specs=[pl.BlockSpec(memory_space=pl.ANY),
                  per_seq(new_nkv), per_seq(win_state), per_seq(new_win), per_seq(new_win_rows), per_seq(q),
                  per_seq(gates), whole(pe), whole(w1), whole(w2)],
        out_specs=[pl.BlockSpec((1, N_HEADS_NSA, HEAD_DIM), lambda b, pt: (b, 0, 0)), per_seq(win_state)],
        scratch_shapes=[pltpu.VMEM((2, CMP_STRIDE, chunks, slabs, HEAD_DIM), F32), pltpu.SemaphoreType.DMA((2,))],
    )
    return pl.pallas_call(
        functools.partial(_nsa_sample_body, n_pages=n_pages, past=past, scale=HEAD_DIM ** -0.5),
        grid_spec=grid_spec,
        out_shape=[jax.ShapeDtypeStruct((s, N_HEADS_NSA, HEAD_DIM), F32),
                   jax.ShapeDtypeStruct(win_state.shape, win_state.dtype)],
        compiler_params=_params("arbitrary"),
        name="nsa_sample",
    )(page_table, cache, new_nkv, win_state, new_win, new_win_rows, q, gates, pe, w1, w2)


def _rope_tables(pos):
    half = HEAD_DIM // 2
    inv_freq = ROPE_THETA ** (-jnp.arange(half, dtype=F32) / half)
    ang = pos.astype(F32)[:, None] * inv_freq[None, :]
    cos, sin = jnp.cos(ang), jnp.sin(ang)
    return jnp.concatenate([cos, cos], axis=1), jnp.concatenate([-sin, sin], axis=1)


def _prepare_weights(w_in, cmp_k_pe, cmp_k_w1, cmp_k_w2, cmp_v_pe, cmp_v_w1, cmp_v_w2,
                     w_branch_sb, w_branch_nsa, w_out, w_gate, w_up, w_down):
    d_sb = N_HEADS_SB * HEAD_DIM
    d_nsa = N_HEADS_NSA * HEAD_DIM
    d_kv = N_KV_NSA * HEAD_DIM
    cuts = [0, d_sb, 3 * d_sb, 3 * d_sb + d_nsa, 3 * d_sb + d_nsa + 4 * d_kv, 3 * d_sb + d_nsa + 6 * d_kv]
    w_in = w_in.astype(BF16)
    n_gate = 3 * N_HEADS_NSA
    w_ng = w_in[:, cuts[5]:cuts[5] + n_gate].reshape(-1, N_KV_NSA, 3 * NSA_GROUP)
    w_ng = jnp.pad(w_ng, ((0, 0), (0, 0), (0, LANES - 3 * NSA_GROUP))).reshape(-1, N_KV_NSA * LANES)
    half = CMP_STRIDE * HEAD_DIM
    halves = lambda w1: w1.reshape(2, half, -1)
    return dict(
        w_in=w_in,
        cuts=dict(sb_q=(cuts[0], cuts[1]), sb_kv=(cuts[1], cuts[2]), nsa_q=(cuts[2], cuts[3]),
                  nsa_kv=(cuts[3], cuts[4]), nsa_win=(cuts[4], cuts[5])),
        nsa_gate=w_ng, merge=w_in[:, cuts[5] + n_gate:],
        pe=jnp.stack([cmp_k_pe.reshape(2, 1, half), cmp_v_pe.reshape(2, 1, half)]),
        w1=jnp.stack([halves(cmp_k_w1), halves(cmp_v_w1)]).astype(BF16),
        w2=jnp.stack([cmp_k_w2, cmp_v_w2]).astype(BF16),
        branch_sb=w_branch_sb.astype(BF16), branch_nsa=w_branch_nsa.astype(BF16), out=w_out.astype(BF16),
        gate=w_gate.astype(BF16), up=w_up.astype(BF16), down=w_down.astype(BF16),
    )


def _project(x, g_attn, w, cos, sin, tm, q_dtype):
    h = _rmsnorm(x, g_attn, BF16, min(tm, 256))
    mm = functools.partial(_matmul, h, tm=tm)
    seg = lambda s, dt: mm(w["w_in"], dt, tn=512, cols=w["cuts"][s], name="proj_" + s)
    rope = lambda s, dt, slabs: _matmul_rope(h, w["w_in"], cos, sin, dt, tm=tm, tn=4 * LANES, rope_slabs=slabs,
                                             cols=w["cuts"][s], name="proj_" + s)
    k_then_v = (True, True, False, False)
    return dict(
        sb_q=seg("sb_q", q_dtype), sb_kv=seg("sb_kv", F32),
        nsa_q=rope("nsa_q", q_dtype, (True,) * 4), nsa_kv=rope("nsa_kv", F32, k_then_v),
        nsa_win=rope("nsa_win", F32, k_then_v),
        nsa_gate=mm(w["nsa_gate"], F32, tn=256, name="proj_nsa_gate"),
        merge=mm(w["merge"], F32, tn=512, name="proj_merge"),
    )


def _finish(x, o_sb, o_nsa, merge, w, g_ffn, g_final, tm):
    merged = _matmul_merge(o_sb, w["branch_sb"], o_nsa, w["branch_nsa"], merge, tm=tm, tn=512, name="branch_merge")
    x = _matmul_residual(merged, w["out"], x, tm=tm, tn=512, tk=merged.shape[1], name="out_proj")
    h = _rmsnorm(x, g_ffn, BF16, min(tm, 256))
    ff = _matmul_gate_up(h, w["gate"], w["up"], tm=tm, tn=512, name="ffn_gate_up")
    x = _matmul_residual(ff, w["down"], x, tm=tm, tn=512, tk=ff.shape[1] // 2, name="ffn_down")
    return _rmsnorm(x, g_final, F32, min(tm, 256))


def kernel(x_prompt, x_sample, cache_sb_kv, cache_nsa_kv, state_nsa_win, page_table, g_attn, w_in, cmp_k_pe, cmp_k_w1, cmp_k_w2, cmp_v_pe, cmp_v_w1, cmp_v_w2, w_branch_sb, w_branch_nsa, w_out, g_ffn, w_gate, w_up, w_down, g_final):
    depth = w_in.shape[0]
    assert depth == 1
    batch, seq, d_model = x_prompt.shape
    n_dec = x_sample.shape[0]
    assert x_sample.shape[1] == 1
    n_pages = page_table.shape[1]
    page = cache_sb_kv.shape[2]
    past = n_pages * page
    win_len = min(WINDOW, seq)
    layer = 0
    w = _prepare_weights(w_in[layer], cmp_k_pe[layer], cmp_k_w1[layer], cmp_k_w2[layer], cmp_v_pe[layer],
                         cmp_v_w1[layer], cmp_v_w2[layer], w_branch_sb[layer], w_branch_nsa[layer], w_out[layer],
                         w_gate[layer], w_up[layer], w_down[layer])

    xp = x_prompt.reshape(batch * seq, d_model)
    cos_p, sin_p = _rope_tables(jnp.arange(seq, dtype=I32))
    pp = _project(xp, g_attn[layer], w, cos_p, sin_p, 1024, BF16)
    o_sb = _sb_prompt(pp["sb_q"], pp["sb_kv"], batch=batch, seq=seq)
    cmp_p = _compress_prompt(pp["nsa_kv"], w["pe"], w["w1"], w["w2"], batch=batch, seq=seq)
    o_nsa = _nsa_prompt(pp["nsa_q"], cmp_p, pp["nsa_kv"], pp["nsa_win"], pp["nsa_gate"], batch=batch, seq=seq)
    y_prompt = _finish(xp, o_sb, o_nsa, pp["merge"], w, g_ffn[layer], g_final, 1024)

    xs = x_sample.reshape(n_dec, d_model)
    cos_s, sin_s = _rope_tables(jnp.full((n_dec,), past, dtype=I32))
    ps = _project(xs, g_attn[layer], w, cos_s, sin_s, n_dec, F32)
    o_sb_s = _sb_sample(ps["sb_q"].reshape(n_dec, 1, -1), cache_sb_kv, page_table)
    nsa_cache = cache_nsa_kv[layer].reshape(cache_nsa_kv.shape[1], page // CMP_STRIDE, CMP_STRIDE, -1, HEAD_DIM)
    gate_s = ps["nsa_gate"].reshape(n_dec, N_KV_NSA, LANES)[:, :, :3 * NSA_GROUP].reshape(n_dec, N_HEADS_NSA, 3)
    gate_s = jnp.pad(gate_s, ((0, 0), (0, 0), (0, LANES - 3)))
    win_state = state_nsa_win[layer].reshape(n_dec, -1, HEAD_DIM)
    o_nsa_s, new_win_s = _nsa_sample(ps["nsa_q"].reshape(n_dec, N_HEADS_NSA, HEAD_DIM), gate_s,
                                     ps["nsa_kv"].reshape(n_dec, 1, -1), win_state,
                                     ps["nsa_win"].reshape(n_dec, 1, -1), nsa_cache, page_table,
                                     w["pe"], w["w1"], w["w2"])
    y_sample = _finish(xs, o_sb_s.reshape(n_dec, -1).astype(BF16), o_nsa_s.reshape(n_dec, -1).astype(BF16),
                       ps["merge"], w, g_ffn[layer], g_final, n_dec)
    return (
        y_prompt.reshape(batch, seq, d_model),
        y_sample.reshape(n_dec, 1, d_model),
        pp["sb_kv"].reshape(depth, batch, seq, 2, N_HEADS_SB, HEAD_DIM),
        ps["sb_kv"].reshape(depth, n_dec, 1, 2, N_HEADS_SB, HEAD_DIM),
        pp["nsa_kv"].reshape(depth, batch, seq, 4, N_KV_NSA, HEAD_DIM),
        ps["nsa_kv"].reshape(depth, n_dec, 1, 4, N_KV_NSA, HEAD_DIM),
        pp["nsa_win"].reshape(batch, seq, -1)[:, seq - win_len:].reshape(depth, batch, win_len, 2, N_KV_NSA, HEAD_DIM),
        new_win_s.reshape(state_nsa_win.shape),
    )
```
